```python
import math
import jax, jax.numpy as jnp
from jax import lax
import numpy as np

D_MODEL = 2048
BATCH = 4
SEQ = 2048
DEPTH = 1
DEC_BATCH = 128
DEC_SEQ = 4
PAST_LEN = 16384
PAGE_SIZE = 128

N_META = 16
H_A = 8
KVH_A = 2
DH_A = 128
REP_A = H_A // KVH_A
H_IDX = 16
D_IDX = 64
TOPK_MAX = 256
IDX_W_SCALE = (H_IDX ** -0.5) * (D_IDX ** -0.5)
N_BUCKETS = 32
MAX_DIST = 128
H_B = 8
Q_LORA = 512
KV_LORA = 256
D_NOPE = 128
D_ROPE = 64
D_V = 128
ROPE_THETA = 10000.0
MLA_SCALE = (D_NOPE + D_ROPE) ** -0.5
N_GROUPS = 4
E_PER_GROUP = 8
N_EXPERTS = N_GROUPS * E_PER_GROUP
TOP_E = 2
D_EXPERT = 512

Q_BLOCK = 128
EPS = 1e-6
IN_SIZES = (H_A * DH_A, KVH_A * DH_A, KVH_A * DH_A, H_IDX * D_IDX, H_IDX, D_IDX,
            Q_LORA, KV_LORA, D_ROPE, D_MODEL, D_MODEL)
N_IN = sum(IN_SIZES)
STATE_IDX = (3, 4, 5, 8, 9)

kernel_name = "hybrid_dsa_mla_hmoe_step"


def _rmsnorm(x, g):
    xf = x.astype(jnp.float32)
    y = xf * lax.rsqrt(jnp.mean(xf * xf, axis=-1, keepdims=True) + EPS)
    return (y * g.astype(jnp.float32)).astype(x.dtype)


def _rope_angles(pos):
    inv = ROPE_THETA ** (-jnp.arange(0, D_ROPE, 2, dtype=jnp.float32) / D_ROPE)
    return pos.astype(jnp.float32)[:, None] * inv[None, :]


def _rope(x, ang):
    cos = jnp.cos(ang).astype(x.dtype)
    sin = jnp.sin(ang).astype(x.dtype)
    x1, x2 = jnp.split(x, 2, axis=-1)
    return jnp.concatenate([x1 * cos - x2 * sin, x2 * cos + x1 * sin], axis=-1)


def _rel_bucket(dist):
    n = jnp.maximum(dist, 0)
    max_exact = N_BUCKETS // 2
    nf = jnp.maximum(n, 1).astype(jnp.float32)
    large = max_exact + (jnp.log(nf / max_exact) / math.log(MAX_DIST / max_exact)
                         * (N_BUCKETS - max_exact)).astype(jnp.int32)
    large = jnp.minimum(large, N_BUCKETS - 1)
    return jnp.where(n < max_exact, n, large)


def _indexer_topk(qi, wi, ki, q_pos, topk):
    s = jnp.einsum('thd,sd->tsh', qi, ki)
    score = jnp.einsum('tsh,th->ts', jax.nn.relu(s), wi).astype(jnp.float32)
    k_pos = jnp.arange(ki.shape[0], dtype=jnp.int32)
    score = jnp.where(k_pos[None, :] <= q_pos[:, None], score, -jnp.inf)
    return lax.top_k(score, topk)[1]


def _sparse_gqa(q, k_sel, v_sel, idx, q_pos, rel_bias):
    T, K = idx.shape
    qg = q.reshape(T, KVH_A, REP_A, DH_A)
    logits = jnp.einsum('tgrd,tkgd->tgrk', qg, k_sel).astype(jnp.float32) * (DH_A ** -0.5)
    bias = rel_bias[_rel_bucket(q_pos[:, None] - idx)].astype(jnp.float32)
    bias = jnp.transpose(bias, (0, 2, 1)).reshape(T, KVH_A, REP_A, K)
    valid = (idx <= q_pos[:, None])[:, None, None, :]
    p = jax.nn.softmax(jnp.where(valid, logits + bias, -jnp.inf), axis=-1)
    out = jnp.einsum('tgrk,tkgd->tgrd', p.astype(v_sel.dtype), v_sel)
    return out.reshape(T, H_A, DH_A)


def _mla_attend(q_lat, q_rope, c, kr, q_pos):
    logits = (jnp.einsum('thr,sr->ths', q_lat, c)
              + jnp.einsum('the,se->ths', q_rope, kr)).astype(jnp.float32) * MLA_SCALE
    k_pos = jnp.arange(c.shape[0], dtype=jnp.int32)
    logits = jnp.where(k_pos[None, None, :] <= q_pos[:, None, None], logits, -jnp.inf)
    p = jax.nn.softmax(logits, axis=-1)
    return jnp.einsum('ths,sr->thr', p.astype(c.dtype), c)


def _query_blocks(fn, q_args, T):
    n_blk = -(-T // Q_BLOCK)
    pad = n_blk * Q_BLOCK - T
    blocks = [jnp.pad(a, [(0, pad)] + [(0, 0)] * (a.ndim - 1)).reshape((n_blk, Q_BLOCK) + a.shape[1:])
              for a in q_args]
    pos = jnp.arange(n_blk * Q_BLOCK, dtype=jnp.int32).reshape(n_blk, Q_BLOCK)
    out = lax.map(lambda xs: fn(*xs), (*blocks, pos))
    return out.reshape((n_blk * Q_BLOCK,) + out.shape[2:])[:T]


def _project(n, ang, w_in, g_q, w_uq, g_kv, w_uk):
    B, T, _ = n.shape
    offs, acc = [], 0
    for s in IN_SIZES[:-1]:
        acc += s
        offs.append(acc)
    qa, ka, va, qi, wi, ki, dq, dkv, krr, ga, gb = jnp.split(n @ w_in, offs, axis=-1)
    q = (_rmsnorm(dq, g_q) @ w_uq).reshape(B, T, H_B, D_NOPE + D_ROPE)
    q_lat = jnp.einsum('bthn,rhn->bthr', q[..., :D_NOPE], w_uk)
    q_rope = _rope(q[..., D_NOPE:], ang[None, :, None, :])
    mix = (qa.reshape(B, T, H_A, DH_A), qi.reshape(B, T, H_IDX, D_IDX), wi * IDX_W_SCALE,
           ka.reshape(B, T, KVH_A, DH_A), va.reshape(B, T, KVH_A, DH_A), ki,
           q_lat, q_rope, _rmsnorm(dkv, g_kv), _rope(krr, ang[None]))
    return mix, ga, gb


def _merge(oa, ob_lat, ga, gb, w_uv, w_oa, w_ob, w_o):
    B, T = oa.shape[:2]
    ya = oa.reshape(B, T, H_A * DH_A) @ w_oa
    yb = jnp.einsum('bthr,rhv->bthv', ob_lat, w_uv).reshape(B, T, H_B * D_V) @ w_ob
    return (jax.nn.sigmoid(ga) * ya + jax.nn.sigmoid(gb) * yb) @ w_o


def _moe(x, w_rg, w_re, w_gate, w_up, w_down):
    shp = x.shape
    xt = x.reshape(-1, shp[-1])
    pg = jax.nn.softmax((xt @ w_rg).astype(jnp.float32), axis=-1)
    g_top, g_idx = lax.top_k(pg, 1)
    le = (xt @ w_re).astype(jnp.float32).reshape(-1, N_GROUPS, E_PER_GROUP)
    le = jnp.take_along_axis(le, g_idx[:, :, None], axis=1)[:, 0]
    e_top, e_idx = lax.top_k(jax.nn.softmax(le, axis=-1), TOP_E)
    wts = g_top * e_top / jnp.sum(e_top, axis=-1, keepdims=True)
    eid = g_idx * E_PER_GROUP + e_idx
    gate = jnp.sum(jax.nn.one_hot(eid, N_EXPERTS, dtype=jnp.float32) * wts[..., None], axis=1)
    h = jax.nn.silu(jnp.einsum('nd,edf->nef', xt, w_gate)) * jnp.einsum('nd,edf->nef', xt, w_up)
    h = h * gate[..., None].astype(h.dtype)
    return jnp.einsum('nef,efd->nd', h, w_down).reshape(shp)


def _prompt_attend(mix, rel_bias, topk):
    def per_seq(qa, qi, wi, ka, va, ki, ql, qr, ckv, kr):
        T = qa.shape[0]

        def dsa_blk(qa_b, qi_b, wi_b, pos_b):
            idx = _indexer_topk(qi_b, wi_b, ki, pos_b, topk)
            return _sparse_gqa(qa_b, ka[idx], va[idx], idx, pos_b, rel_bias)

        def mla_blk(ql_b, qr_b, pos_b):
            return _mla_attend(ql_b, qr_b, ckv, kr, pos_b)

        return (_query_blocks(dsa_blk, (qa, qi, wi), T), _query_blocks(mla_blk, (ql, qr), T))

    return jax.vmap(per_seq)(*mix)


def _sample_attend(mix, page_table, cache_k, cache_v, cache_idx_k, cache_ckv, cache_kr, layer, rel_bias, topk):
    T = mix[0].shape[1]
    past = page_table.shape[1] * PAGE_SIZE
    q_pos = past + jnp.arange(T, dtype=jnp.int32)

    def one(args):
        qa, qi, wi, ka, va, ki, ql, qr, ckv, kr, pt = args
        ki_all = jnp.concatenate([cache_idx_k[layer, pt].reshape(past, D_IDX), ki], axis=0)
        idx = _indexer_topk(qi, wi, ki_all, q_pos, topk)
        in_past = idx < past
        pidx = jnp.minimum(idx, past - 1)
        phys = pt[pidx // PAGE_SIZE]
        off = pidx % PAGE_SIZE
        nidx = jnp.clip(idx - past, 0, T - 1)
        sel = in_past[..., None, None]
        k_sel = jnp.where(sel, cache_k[layer, phys, off], ka[nidx])
        v_sel = jnp.where(sel, cache_v[layer, phys, off], va[nidx])
        oa = _sparse_gqa(qa, k_sel, v_sel, idx, q_pos, rel_bias)
        c_all = jnp.concatenate([cache_ckv[layer, pt].reshape(past, KV_LORA), ckv], axis=0)
        kr_all = jnp.concatenate([cache_kr[layer, pt].reshape(past, D_ROPE), kr], axis=0)
        ob = _mla_attend(ql, qr, c_all, kr_all, q_pos)
        return oa, ob

    return lax.map(one, (*mix, page_table))


def setup_inputs(seed: int = 0) -> dict:
    key = jax.random.key(seed)
    ks = iter(jax.random.split(key, 32))

    def nrm(shape, scale):
        return jax.random.normal(next(ks), shape, jnp.float32) * scale

    n_pages = PAST_LEN // PAGE_SIZE
    n_used = DEC_BATCH * n_pages
    n_pool = n_used + max(1, n_used // 4)
    perm = jax.random.permutation(next(ks), n_pool)
    page_table = perm[:n_used].reshape(DEC_BATCH, n_pages).astype(jnp.int32)
    D = D_MODEL
    return {
        "x_prompt": nrm((BATCH, SEQ, D), 1.0),
        "x_sample": nrm((DEC_BATCH, DEC_SEQ, D), 1.0),
        "cache_k": nrm((DEPTH, n_pool, PAGE_SIZE, KVH_A, DH_A), 1.0),
        "cache_v": nrm((DEPTH, n_pool, PAGE_SIZE, KVH_A, DH_A), 1.0),
        "cache_idx_k": nrm((DEPTH, n_pool, PAGE_SIZE, D_IDX), 1.0),
        "cache_ckv": nrm((DEPTH, n_pool, PAGE_SIZE, KV_LORA), 1.0),
        "cache_kr": nrm((DEPTH, n_pool, PAGE_SIZE, D_ROPE), 1.0),
        "page_table": page_table,
        "meta_tokens": nrm((N_META, D), 1.0),
        "rel_bias": nrm((N_BUCKETS, H_A), 0.5),
        "g_attn": 1.0 + nrm((DEPTH, D), 0.02),
        "w_in": nrm((DEPTH, D, N_IN), D ** -0.5),
        "g_q": 1.0 + nrm((DEPTH, Q_LORA), 0.02),
        "w_uq": nrm((DEPTH, Q_LORA, H_B * (D_NOPE + D_ROPE)), Q_LORA ** -0.5),
        "g_kv": 1.0 + nrm((DEPTH, KV_LORA), 0.02),
        "w_uk": nrm((DEPTH, KV_LORA, H_B, D_NOPE), KV_LORA ** -0.5),
        "w_uv": nrm((DEPTH, KV_LORA, H_B, D_V), KV_LORA ** -0.5),
        "w_oa": nrm((DEPTH, H_A * DH_A, D), (H_A * DH_A) ** -0.5),
        "w_ob": nrm((DEPTH, H_B * D_V, D), (H_B * D_V) ** -0.5),
        "w_o": nrm((DEPTH, D, D), D ** -0.5),
        "g_ffn": 1.0 + nrm((DEPTH, D), 0.02),
        "w_rg": nrm((DEPTH, D, N_GROUPS), D ** -0.5),
        "w_re": nrm((DEPTH, D, N_EXPERTS), D ** -0.5),
        "w_gate": nrm((DEPTH, N_EXPERTS, D, D_EXPERT), D ** -0.5),
        "w_up": nrm((DEPTH, N_EXPERTS, D, D_EXPERT), D ** -0.5),
        "w_down": nrm((DEPTH, N_EXPERTS, D_EXPERT, D), D_EXPERT ** -0.5),
        "g_final": 1.0 + nrm((D,), 0.02),
    }


def reference(x_prompt, x_sample, cache_k, cache_v, cache_idx_k, cache_ckv, cache_kr, page_table,
              meta_tokens, rel_bias, g_attn, w_in, g_q, w_uq, g_kv, w_uk, w_uv, w_oa, w_ob, w_o,
              g_ffn, w_rg, w_re, w_gate, w_up, w_down, g_final):
    B, S, D = x_prompt.shape
    T_s = x_sample.shape[1]
    past = page_table.shape[1] * PAGE_SIZE
    topk_p = min(TOPK_MAX, S // 4)
    topk_s = min(TOPK_MAX, (past + T_s) // 4)
    hp = jnp.concatenate([jnp.broadcast_to(meta_tokens.astype(x_prompt.dtype)[None], (B, N_META, D)),
                          x_prompt], axis=1)
    hs = x_sample
    ang_p = _rope_angles(jnp.arange(S + N_META, dtype=jnp.int32))
    ang_s = _rope_angles(past + jnp.arange(T_s, dtype=jnp.int32))
    new_p, new_s = [], []
    for l in range(DEPTH):
        proj_w = (w_in[l], g_q[l], w_uq[l], g_kv[l], w_uk[l])
        out_w = (w_uv[l], w_oa[l], w_ob[l], w_o[l])
        moe_w = (w_rg[l], w_re[l], w_gate[l], w_up[l], w_down[l])
        mix_p, ga_p, gb_p = _project(_rmsnorm(hp, g_attn[l]), ang_p, *proj_w)
        oa_p, ob_p = _prompt_attend(mix_p, rel_bias, topk_p)
        hp = hp + _merge(oa_p, ob_p, ga_p, gb_p, *out_w)
        hp = hp + _moe(_rmsnorm(hp, g_ffn[l]), *moe_w)
        mix_s, ga_s, gb_s = _project(_rmsnorm(hs, g_attn[l]), ang_s, *proj_w)
        oa_s, ob_s = _sample_attend(mix_s, page_table, cache_k, cache_v, cache_idx_k, cache_ckv,
                                    cache_kr, l, rel_bias, topk_s)
        hs = hs + _merge(oa_s, ob_s, ga_s, gb_s, *out_w)
        hs = hs + _moe(_rmsnorm(hs, g_ffn[l]), *moe_w)
        new_p.append([mix_p[i] for i in STATE_IDX])
        new_s.append([mix_s[i] for i in STATE_IDX])
    y_prompt = _rmsnorm(hp, g_final)[:, N_META:]
    y_sample = _rmsnorm(hs, g_final)
    nk_p, nv_p, ni_p, nc_p, nr_p = [jnp.stack([st[i] for st in new_p]) for i in range(5)]
    nk_s, nv_s, ni_s, nc_s, nr_s = [jnp.stack([st[i] for st in new_s]) for i in range(5)]
    return (y_prompt, y_sample, nk_p, nv_p, ni_p, nc_p, nr_p, nk_s, nv_s, ni_s, nc_s, nr_s)
```

```python
import functools
import math

import numpy as np
import jax
import jax.numpy as jnp
from jax import lax
from jax.experimental import pallas as pl
from jax.experimental.pallas import tpu as pltpu

N_META = 16
H_A, KVH_A, DH_A = 8, 2, 128
REP_A = H_A // KVH_A
H_IDX, D_IDX = 16, 64
TOPK_MAX = 256
IDX_W_SCALE = (H_IDX ** -0.5) * (D_IDX ** -0.5)
N_BUCKETS, MAX_DIST = 32, 128
H_B, Q_LORA, KV_LORA, D_NOPE, D_ROPE, D_V = 8, 512, 256, 128, 64, 128
ROPE_THETA = 10000.0
MLA_SCALE = (D_NOPE + D_ROPE) ** -0.5
DSA_SCALE = DH_A ** -0.5
N_GROUPS, E_PER_GROUP, TOP_E, D_EXPERT = 4, 8, 2, 512
N_EXPERTS = N_GROUPS * E_PER_GROUP
EPS = 1e-6
PAGE = 128

LANE = 128
VMEM_LIMIT_BYTES = 56 * 1024 * 1024

QB = 128
ROW_TILE = 256
MLA_KCAT = KV_LORA + LANE
BAND = 2 * QB
INT_MIN = -2 ** 31
NEG_INF_KEY = -2 ** 31 + 0x7FFFFF

_SEGS = (("ga", 2048), ("gb", 2048), ("qa", 1024), ("qi", 1024), ("dq", 512), ("ka", 256),
         ("va", 256), ("dkv", 256), ("wi", 128), ("ki", 128), ("krr", 128), ("krs", 128))
_COL = {}
_o = 0
for _n, _w in _SEGS:
    assert _o % _w == 0
    _COL[_n] = (_o // _w, _w)
    _o += _w
PROJ_W = 8192
assert _o <= PROJ_W

f32, bf16, i32 = jnp.float32, jnp.bfloat16, jnp.int32


def _cparams(*sem):
    return pltpu.CompilerParams(dimension_semantics=sem, vmem_limit_bytes=VMEM_LIMIT_BYTES)


def _dot(a, b):
    return jnp.dot(a, b, preferred_element_type=f32)


def _dot_nt(a, b):
    return lax.dot_general(a, b, (((1,), (1,)), ((), ())), preferred_element_type=f32)


def _rms(x, g):
    return x * lax.rsqrt(jnp.mean(x * x, axis=-1, keepdims=True) + EPS) * g


def _const_spec(shape):
    nd = len(shape)
    return pl.BlockSpec(shape, lambda *_: (0,) * nd, pipeline_mode=pl.Buffered(1))


def _norm_matmul_kernel(x_ref, g_ref, w_ref, o_ref, xn_ref):
    @pl.when(pl.program_id(1) == 0)
    def _():
        xn_ref[...] = _rms(x_ref[...], g_ref[...]).astype(bf16)

    o_ref[...] = _dot(xn_ref[...], w_ref[...])


def _norm_matmul(x, g, w, tm, tn):
    n, k = x.shape
    m = w.shape[1]
    return pl.pallas_call(
        _norm_matmul_kernel,
        grid=(n // tm, m // tn),
        in_specs=[pl.BlockSpec((tm, k), lambda i, j: (i, 0)),
                  pl.BlockSpec((1, k), lambda i, j: (0, 0)),
                  pl.BlockSpec((k, tn), lambda i, j: (0, j))],
        out_specs=pl.BlockSpec((tm, tn), lambda i, j: (i, j)),
        out_shape=jax.ShapeDtypeStruct((n, m), f32),
        scratch_shapes=[pltpu.VMEM((tm, k), bf16)],
        compiler_params=_cparams("parallel", "arbitrary"),
        name="in_proj",
    )(x, g.reshape(1, k), w)


def _post_kernel(dq_ref, dkv_ref, krr_ref, krs_ref, ka_ref, va_ref, ki_ref, cos_ref, sin_ref,
                 gq_ref, gkv_ref, wuq_ref, wuk_ref,
                 qcat_ref, ckv_ref, kr_ref, kcat_ref, kab_ref, vab_ref, kib_ref):
    tm = dq_ref.shape[0]
    cos_t = cos_ref[...]
    sin_t = sin_ref[...]
    dqn = _rms(dq_ref[...], gq_ref[...]).astype(bf16)
    q = _dot(dqn, wuq_ref[...])
    for h in range(H_B):
        qlat = _dot(q[:, h * 128:(h + 1) * 128].astype(bf16), wuk_ref[h])
        qr = (q[:, 1024 + h * 128:1024 + (h + 1) * 128] * cos_t
              + q[:, 2048 + h * 128:2048 + (h + 1) * 128] * sin_t)
        for sb in range(tm // QB):
            qcat_ref[sb, h, :, 0:KV_LORA] = qlat[sb * QB:(sb + 1) * QB].astype(bf16)
            qcat_ref[sb, h, :, KV_LORA:MLA_KCAT] = qr[sb * QB:(sb + 1) * QB].astype(bf16)
    ckv = _rms(dkv_ref[...], gkv_ref[...])
    ckv_ref[...] = ckv
    kr = krr_ref[...] * cos_t + krs_ref[...] * sin_t
    kr_ref[...] = kr
    kcat_ref[:, 0:KV_LORA] = ckv.astype(bf16)
    kcat_ref[:, KV_LORA:MLA_KCAT] = kr.astype(bf16)
    kab_ref[...] = ka_ref[...].astype(bf16)
    vab_ref[...] = va_ref[...].astype(bf16)
    kib_ref[...] = ki_ref[...].astype(bf16)


def _post_project(proj, cos_t, sin_t, g_q, g_kv, wuq_p, wuk_t):
    n = proj.shape[0]
    tm = ROW_TILE

    def col(name):
        j, w = _COL[name]
        return pl.BlockSpec((tm, w), lambda i, j=j: (i, j))

    row = lambda w: pl.BlockSpec((tm, w), lambda i: (i, 0))
    outs = pl.pallas_call(
        _post_kernel,
        grid=(n // tm,),
        in_specs=[col("dq"), col("dkv"), col("krr"), col("krs"), col("ka"), col("va"), col("ki"),
                  row(LANE), row(LANE),
                  _const_spec((1, Q_LORA)), _const_spec((1, KV_LORA)),
                  _const_spec(wuq_p.shape), _const_spec(wuk_t.shape)],
        out_specs=[pl.BlockSpec((tm // QB, H_B, QB, MLA_KCAT), lambda i: (i, 0, 0, 0)),
                   row(KV_LORA), row(LANE), row(MLA_KCAT), row(256), row(256), row(LANE)],
        out_shape=[jax.ShapeDtypeStruct((n // QB, H_B, QB, MLA_KCAT), bf16),
                   jax.ShapeDtypeStruct((n, KV_LORA), f32),
                   jax.ShapeDtypeStruct((n, LANE), f32),
                   jax.ShapeDtypeStruct((n, MLA_KCAT), bf16),
                   jax.ShapeDtypeStruct((n, 256), bf16),
                   jax.ShapeDtypeStruct((n, 256), bf16),
                   jax.ShapeDtypeStruct((n, LANE), bf16)],
        compiler_params=_cparams("parallel"),
        name="post_proj",
    )(proj, proj, proj, proj, proj, proj, proj, cos_t, sin_t,
      g_q.reshape(1, -1), g_kv.reshape(1, -1), wuq_p, wuk_t)
    return outs


def _mla_prompt_kernel(q_ref, k_ref, o_ref, *, tk):
    i = pl.program_id(1)
    tp = k_ref.shape[0]
    t0 = i * QB
    rows = H_B * QB
    q = q_ref[0].reshape(rows, MLA_KCAT)
    qpos = t0 + (lax.broadcasted_iota(i32, (rows, 1), 0) & (QB - 1))
    nk = (t0 + QB + tk - 1) // tk

    def body(c, carry):
        m, l, acc = carry
        lo = c * tk
        st = pl.multiple_of(jnp.minimum(lo, tp - tk), QB)
        ks = k_ref[pl.ds(st, tk), :]
        s = _dot_nt(q, ks) * MLA_SCALE
        kpos = st + lax.broadcasted_iota(i32, (1, tk), 1)
        s = jnp.where((kpos <= qpos) & (kpos >= lo), s, -jnp.inf)
        m_new = jnp.maximum(m, jnp.max(s, axis=1, keepdims=True))
        p = jnp.exp(s - m_new)
        alpha = jnp.exp(m - m_new)
        l = alpha * l + jnp.sum(p, axis=1, keepdims=True)
        acc = alpha * acc + _dot(p.astype(bf16), ks[:, 0:KV_LORA])
        return m_new, l, acc

    m0 = jnp.full((rows, 1), -jnp.inf, f32)
    l0 = jnp.zeros((rows, 1), f32)
    a0 = jnp.zeros((rows, KV_LORA), f32)
    _, l, acc = lax.fori_loop(0, nk, body, (m0, l0, a0))
    o = acc / l
    for h in range(H_B):
        o_ref[:, h * KV_LORA:(h + 1) * KV_LORA] = o[h * QB:(h + 1) * QB].astype(bf16)


def _mla_prompt(qcat, kcat, nb, tp):
    nq = tp // QB
    tk = min(512, tp)
    return pl.pallas_call(
        functools.partial(_mla_prompt_kernel, tk=tk),
        grid=(nb, nq),
        in_specs=[pl.BlockSpec((1, H_B, QB, MLA_KCAT), lambda b, i: (b * nq + i, 0, 0, 0)),
                  pl.BlockSpec((tp, MLA_KCAT), lambda b, i: (b, 0))],
        out_specs=pl.BlockSpec((QB, H_B * KV_LORA), lambda b, i: (b * nq + i, 0)),
        out_shape=jax.ShapeDtypeStruct((nb * tp, H_B * KV_LORA), bf16),
        compiler_params=_cparams("parallel", "arbitrary"),
        name="mla_prompt",
    )(qcat, kcat)


def _sortable(score):
    b = lax.bitcast_convert_type(score, i32)
    return jnp.where(b < 0, b ^ jnp.int32(0x7FFFFFFF), b)


def _kth_largest(key_ref, k, count_axes, shape1):
    def body(it, res_u):
        cand_u = res_u | lax.shift_left(jnp.int32(1), 31 - it)
        cand_s = cand_u ^ jnp.int32(INT_MIN)
        ge = (key_ref[...] >= cand_s).astype(i32)
        cnt = jnp.sum(ge, axis=count_axes, keepdims=True)
        return jnp.where(cnt >= k, cand_u, res_u)

    res_u = lax.fori_loop(0, 32, body, jnp.zeros(shape1, i32))
    return res_u ^ jnp.int32(INT_MIN)


def _dsa_prompt_kernel(qa_ref, qi_ref, wi_ref, ki_ref, ka_ref, va_ref, band_ref, far_ref, o_ref,
                       key_ref, *, topk):
    i = pl.program_id(1)
    t0 = i * QB
    tp = ki_ref.shape[0]
    ki = ki_ref[:, 0:D_IDX]
    wi = wi_ref[...] * IDX_W_SCALE
    qi = qi_ref[...]
    score = jnp.zeros((QB, tp), f32)
    hs = 4
    for hg in range(H_IDX // hs):
        qs = jnp.concatenate(
            [qi[:, (hs * hg + u) * D_IDX:(hs * hg + u + 1) * D_IDX] for u in range(hs)], axis=0)
        s = _dot_nt(qs.astype(bf16), ki)
        for u in range(hs):
            h = hs * hg + u
            score = score + wi[:, h:h + 1] * jnp.maximum(s[u * QB:(u + 1) * QB], 0.0)
    kpos = lax.broadcasted_iota(i32, (1, tp), 1)
    qpos = t0 + lax.broadcasted_iota(i32, (QB, 1), 0)
    causal = kpos <= qpos
    key_ref[...] = jnp.where(causal, _sortable(score), jnp.int32(NEG_INF_KEY))
    thr = _kth_largest(key_ref, topk, (1,), (QB, 1))
    sel = causal & (key_ref[...] >= thr)

    shift = (t0 + tp - QB) % tp
    qa = qa_ref[...]
    for g in range(KVH_A):
        qg = jnp.concatenate(
            [qa[:, (REP_A * g + r) * DH_A:(REP_A * g + r + 1) * DH_A] for r in range(REP_A)], axis=0)
        lg = _dot_nt(qg.astype(bf16), ka_ref[:, g * DH_A:(g + 1) * DH_A]) * DSA_SCALE
        vg = va_ref[:, g * DH_A:(g + 1) * DH_A]
        for r in range(REP_A):
            h = REP_A * g + r
            bias = band_ref[h]
            if tp > BAND:
                bias = jnp.concatenate(
                    [bias, jnp.broadcast_to(far_ref[h][:, 0:1], (QB, tp - BAND))], axis=1)
            bias = pltpu.roll(bias, shift, axis=1)
            x = jnp.where(sel, lg[r * QB:(r + 1) * QB] + bias, -jnp.inf)
            m = jnp.max(x, axis=1, keepdims=True)
            p = jnp.exp(x - m)
            den = jnp.sum(p, axis=1, keepdims=True)
            o = _dot(p.astype(bf16), vg) / den
            o_ref[:, h * DH_A:(h + 1) * DH_A] = o.astype(bf16)


def _dsa_prompt(proj, kib, kab, vab, band, far, nb, tp, topk):
    nq = tp // QB
    jqa, jqi, jwi = _COL["qa"][0], _COL["qi"][0], _COL["wi"][0]
    return pl.pallas_call(
        functools.partial(_dsa_prompt_kernel, topk=topk),
        grid=(nb, nq),
        in_specs=[pl.BlockSpec((QB, 1024), lambda b, i: (b * nq + i, jqa)),
                  pl.BlockSpec((QB, 1024), lambda b, i: (b * nq + i, jqi)),
                  pl.BlockSpec((QB, LANE), lambda b, i: (b * nq + i, jwi)),
                  pl.BlockSpec((tp, LANE), lambda b, i: (b, 0)),
                  pl.BlockSpec((tp, 256), lambda b, i: (b, 0)),
                  pl.BlockSpec((tp, 256), lambda b, i: (b, 0)),
                  _const_spec((H_A, QB, BAND)),
                  _const_spec((H_A, 1, LANE))],
        out_specs=pl.BlockSpec((QB, H_A * DH_A), lambda b, i: (b * nq + i, 0)),
        out_shape=jax.ShapeDtypeStruct((nb * tp, H_A * DH_A), bf16),
        scratch_shapes=[pltpu.VMEM((QB, tp), i32)],
        compiler_params=_cparams("parallel", "arbitrary"),
        name="dsa_prompt",
    )(proj, proj, proj, kib, kab, vab, band, far)


def _page_copies(pt_ref, base, c, slot, pp, srcs, bufs, sems):
    out = []
    for p in range(pp):
        page = pt_ref[base + c * pp + p]
        for a, (src, buf) in enumerate(zip(srcs, bufs)):
            out.append(pltpu.make_async_copy(src(page), buf(slot, p), sems.at[slot, a]))
    return out


def _online_update(carry, s, v):
    m, l, acc = carry
    m_new = jnp.maximum(m, jnp.max(s, axis=1, keepdims=True))
    m_safe = jnp.where(m_new == -jnp.inf, 0.0, m_new)
    p = jnp.exp(s - m_safe)
    alpha = jnp.exp(m - m_safe)
    l = alpha * l + jnp.sum(p, axis=1, keepdims=True)
    acc = alpha * acc + _dot(p.astype(bf16), v)
    return m_new, l, acc


def _sample_a_kernel(pt_ref, qi_ref, wi_ref, qc_ref, kin_ref, kcn_ref, idx_hbm, ckv_hbm, kr_hbm,
                     sc_ref, ob_ref, ibuf, cbuf, rbuf, sems, *, n_pages, pp, ts):
    s_id = pl.program_id(0)
    n_chunks = n_pages // pp
    ch = pp * PAGE
    base = s_id * n_pages
    srcs = (lambda pg: idx_hbm.at[0, pg], lambda pg: ckv_hbm.at[0, pg], lambda pg: kr_hbm.at[0, pg])
    bufs = (lambda sl, p: ibuf.at[sl, pl.ds(p * PAGE, PAGE)],
            lambda sl, p: cbuf.at[sl, pl.ds(p * PAGE, PAGE)],
            lambda sl, p: rbuf.at[sl, pl.ds(p * PAGE, PAGE)])

    def start(c, slot):
        for cp in _page_copies(pt_ref, base, c, slot, pp, srcs, bufs, sems):
            cp.start()

    def wait(c, slot):
        for cp in _page_copies(pt_ref, base, c, slot, pp, srcs, bufs, sems):
            cp.wait()

    qi = qi_ref[0]
    w = wi_ref[0]
    qlat = qc_ref[0][:, 0:KV_LORA]
    qrope = qc_ref[0][:, KV_LORA:KV_LORA + D_ROPE]
    rows = H_B * 8

    def idx_scores(kidx):
        s = jnp.maximum(_dot_nt(qi, kidx), 0.0) * w
        return jnp.sum(s.reshape(H_IDX, 8, s.shape[1]), axis=0)

    start(0, 0)

    def body(c, carry):
        slot = c % 2

        @pl.when(c + 1 < n_chunks)
        def _():
            start(c + 1, 1 - slot)

        wait(c, slot)
        sc_ref[0, c] = idx_scores(ibuf[slot].astype(bf16))
        ck = cbuf[slot].astype(bf16)
        rk = rbuf[slot].astype(bf16)
        s = (_dot_nt(qlat, ck) + _dot_nt(qrope, rk)) * MLA_SCALE
        return _online_update(carry, s, ck)

    carry = (jnp.full((rows, 1), -jnp.inf, f32), jnp.zeros((rows, 1), f32),
             jnp.zeros((rows, KV_LORA), f32))
    carry = lax.fori_loop(0, n_chunks, body, carry)

    kj = lax.broadcasted_iota(i32, (1, PAGE), 1)
    ok8 = (kj <= lax.broadcasted_iota(i32, (8, 1), 0)) & (kj < ts)
    sc_new = jnp.where(ok8, idx_scores(kin_ref[0]), -jnp.inf)
    if ch > PAGE:
        sc_new = jnp.concatenate([sc_new, jnp.full((8, ch - PAGE), -jnp.inf, f32)], axis=1)
    sc_ref[0, n_chunks] = sc_new
    kcn = kcn_ref[0]
    okr = (kj <= (lax.broadcasted_iota(i32, (rows, 1), 0) & 7)) & (kj < ts)
    s = jnp.where(okr, _dot_nt(qc_ref[0], kcn) * MLA_SCALE, -jnp.inf)
    _, l, acc = _online_update(carry, s, kcn[:, 0:KV_LORA])
    ob_ref[0] = acc / l


def _sample_pass_a(page_table, qi_s, wi_s, qc_s, ki_new, kc_new, cache_idx_k, cache_ckv, cache_kr, ts, pp):
    bs, n_pages = page_table.shape
    n_chunks = n_pages // pp
    ch = pp * PAGE
    blk = lambda *shape: pl.BlockSpec((1,) + shape, lambda s, pt: (s,) + (0,) * len(shape))
    any_spec = pl.BlockSpec(memory_space=pl.ANY)
    grid_spec = pltpu.PrefetchScalarGridSpec(
        num_scalar_prefetch=1,
        grid=(bs,),
        in_specs=[blk(H_IDX * 8, D_IDX), blk(H_IDX * 8, 1), blk(H_B * 8, MLA_KCAT),
                  blk(PAGE, D_IDX), blk(PAGE, MLA_KCAT), any_spec, any_spec, any_spec],
        out_specs=[blk(n_chunks + 1, 8, ch), blk(H_B * 8, KV_LORA)],
        scratch_shapes=[pltpu.VMEM((2, ch, D_IDX), f32), pltpu.VMEM((2, ch, KV_LORA), f32),
                        pltpu.VMEM((2, ch, D_ROPE), f32), pltpu.SemaphoreType.DMA((2, 3))])
    return pl.pallas_call(
        functools.partial(_sample_a_kernel, n_pages=n_pages, pp=pp, ts=ts),
        grid_spec=grid_spec,
        out_shape=[jax.ShapeDtypeStruct((bs, n_chunks + 1, 8, ch), f32),
                   jax.ShapeDtypeStruct((bs, H_B * 8, KV_LORA), f32)],
        compiler_params=_cparams("arbitrary"),
        name="sample_idx_mla",
    )(page_table.reshape(-1), qi_s, wi_s, qc_s, ki_new, kc_new, cache_idx_k, cache_ckv, cache_kr)


def _sample_b_kernel(pt_ref, sc_ref, qa_ref, kn_ref, vn_ref, bl_ref, bn_ref, far_ref, k_hbm, v_hbm,
                     o_ref, key_ref, kbuf, vbuf, sems, *, n_pages, pp, topk):
    s_id = pl.program_id(0)
    n_chunks = n_pages // pp
    ch = pp * PAGE
    base = s_id * n_pages
    srcs, bufs = [], []
    for g in range(KVH_A):
        srcs.append(lambda pg, g=g: k_hbm.at[0, pg, :, g, :])
        bufs.append(lambda sl, p, g=g: kbuf.at[sl, g, pl.ds(p * PAGE, PAGE)])
        srcs.append(lambda pg, g=g: v_hbm.at[0, pg, :, g, :])
        bufs.append(lambda sl, p, g=g: vbuf.at[sl, g, pl.ds(p * PAGE, PAGE)])

    def start(c, slot):
        for cp in _page_copies(pt_ref, base, c, slot, pp, srcs, bufs, sems):
            cp.start()

    def wait(c, slot):
        for cp in _page_copies(pt_ref, base, c, slot, pp, srcs, bufs, sems):
            cp.wait()

    start(0, 0)
    key_ref[...] = _sortable(sc_ref[0])
    thr = _kth_largest(key_ref, topk, (0, 2), (1, 8, 1))[0]
    rows = REP_A * 8

    def attend(carry, c_key, kk, vv, bias_of):
        sel = key_ref[c_key] >= thr
        n = sel.shape[1]
        new = []
        for g in range(KVH_A):
            lg = _dot_nt(qa_ref[0, g], kk(g)) * DSA_SCALE
            lg = lg.reshape(REP_A, 8, n) + bias_of(g)
            lg = jnp.where(sel[None], lg, -jnp.inf).reshape(rows, n)
            new.append(_online_update(carry[g], lg, vv(g)))
        return tuple(new)

    def far_bias(g):
        return far_ref[g]

    def body(c, carry):
        slot = c % 2

        @pl.when(c + 1 < n_chunks)
        def _():
            start(c + 1, 1 - slot)

        wait(c, slot)
        return attend(carry, c, lambda g: kbuf[slot, g].astype(bf16),
                      lambda g: vbuf[slot, g].astype(bf16), far_bias)

    init = (jnp.full((rows, 1), -jnp.inf, f32), jnp.zeros((rows, 1), f32), jnp.zeros((rows, DH_A), f32))
    carry = lax.fori_loop(0, n_chunks - 1, body, (init,) * KVH_A)
    last = n_chunks - 1
    slot = last % 2
    wait(last, slot)
    carry = attend(carry, last, lambda g: kbuf[slot, g].astype(bf16),
                   lambda g: vbuf[slot, g].astype(bf16), lambda g: bl_ref[g])
    sel_new = key_ref[n_chunks][:, 0:PAGE] >= thr
    for g in range(KVH_A):
        lg = _dot_nt(qa_ref[0, g], kn_ref[0][:, g * DH_A:(g + 1) * DH_A]) * DSA_SCALE
        lg = lg.reshape(REP_A, 8, PAGE) + bn_ref[g]
        lg = jnp.where(sel_new[None], lg, -jnp.inf).reshape(rows, PAGE)
        _, l, acc = _online_update(carry[g], lg, vn_ref[0][:, g * DH_A:(g + 1) * DH_A])
        o_ref[0, g] = acc / l


def _sample_pass_b(page_table, scores, qa_s, ka_new, va_new, bias_last, bias_new, far_s, cache_k, cache_v,
                   topk, pp):
    bs, n_pages = page_table.shape
    n_chunks = n_pages // pp
    ch = pp * PAGE
    blk = lambda *shape: pl.BlockSpec((1,) + shape, lambda s, pt: (s,) + (0,) * len(shape))
    cst = lambda shape: pl.BlockSpec(shape, lambda s, pt: (0,) * len(shape))
    any_spec = pl.BlockSpec(memory_space=pl.ANY)
    grid_spec = pltpu.PrefetchScalarGridSpec(
        num_scalar_prefetch=1,
        grid=(bs,),
        in_specs=[blk(n_chunks + 1, 8, ch), blk(KVH_A, REP_A * 8, DH_A), blk(PAGE, 256), blk(PAGE, 256),
                  cst((KVH_A, REP_A, 8, ch)), cst((KVH_A, REP_A, 8, PAGE)), cst((KVH_A, REP_A, 1, 1)),
                  any_spec, any_spec],
        out_specs=[blk(KVH_A, REP_A * 8, DH_A)],
        scratch_shapes=[pltpu.VMEM((n_chunks + 1, 8, ch), i32),
                        pltpu.VMEM((2, KVH_A, ch, DH_A), f32), pltpu.VMEM((2, KVH_A, ch, DH_A), f32),
                        pltpu.SemaphoreType.DMA((2, 2 * KVH_A))])
    return pl.pallas_call(
        functools.partial(_sample_b_kernel, n_pages=n_pages, pp=pp, topk=topk),
        grid_spec=grid_spec,
        out_shape=[jax.ShapeDtypeStruct((bs, KVH_A, REP_A * 8, DH_A), f32)],
        compiler_params=_cparams("arbitrary"),
        name="sample_dsa",
    )(page_table.reshape(-1), scores, qa_s, ka_new, va_new, bias_last, bias_new, far_s, cache_k, cache_v)[0]


def _merge_kernel(oa_ref, ob_ref, ga_ref, gb_ref, h_ref, woa_ref, wuv_ref, wob_ref, wo_ref, gf_ref, wr_ref,
                  h1_ref, xn_ref, lg_ref):
    ya = _dot(oa_ref[...], woa_ref[...])
    ob = ob_ref[...]
    obv = jnp.concatenate(
        [_dot(ob[:, h * KV_LORA:(h + 1) * KV_LORA], wuv_ref[h]) for h in range(H_B)], axis=1)
    yb = _dot(obv.astype(bf16), wob_ref[...])
    z = jax.nn.sigmoid(ga_ref[...]) * ya + jax.nn.sigmoid(gb_ref[...]) * yb
    h1 = h_ref[...] + _dot(z.astype(bf16), wo_ref[...])
    h1_ref[...] = h1
    xn = _rms(h1, gf_ref[...])
    xn_ref[...] = xn
    lg_ref[...] = _dot(xn.astype(bf16), wr_ref[...])


def _merge(oa, ob, proj, h, woa, wuv, wob, wo, g_ffn, wr):
    n, d = h.shape
    tm = ROW_TILE
    row = lambda w: pl.BlockSpec((tm, w), lambda i: (i, 0))
    jga, jgb = _COL["ga"][0], _COL["gb"][0]
    return pl.pallas_call(
        _merge_kernel,
        grid=(n // tm,),
        in_specs=[row(oa.shape[1]), row(ob.shape[1]),
                  pl.BlockSpec((tm, d), lambda i: (i, jga)), pl.BlockSpec((tm, d), lambda i: (i, jgb)),
                  row(d), _const_spec(woa.shape), _const_spec(wuv.shape), _const_spec(wob.shape),
                  _const_spec(wo.shape), _const_spec((1, d)), _const_spec(wr.shape)],
        out_specs=[row(d), row(d), row(LANE)],
        out_shape=[jax.ShapeDtypeStruct((n, d), f32), jax.ShapeDtypeStruct((n, d), f32),
                   jax.ShapeDtypeStruct((n, LANE), f32)],
        compiler_params=_cparams("parallel"),
        name="merge_router",
    )(oa, ob, proj, proj, h, woa, wuv, wob, wo, g_ffn.reshape(1, d), wr)


def _row_gather(idx_ref, base, n, src_hbm, dst, sem):
    def issue(r, c):
        pltpu.make_async_copy(src_hbm.at[pl.ds(idx_ref[base + r], 1)], dst.at[pl.ds(r, 1)], sem).start()
        return c

    lax.fori_loop(0, n, issue, 0)

    def drain(r, c):
        pltpu.make_async_copy(src_hbm.at[pl.ds(0, 1)], dst.at[pl.ds(r, 1)], sem).wait()
        return c

    lax.fori_loop(0, n, drain, 0)


def _moe_kernel(te_ref, src_ref, nu_ref, x_hbm, wt_ref, wg_ref, wu_ref, wd_ref, y_ref, xbuf, sem, *, tm):
    t = pl.program_id(0)

    @pl.when(t < nu_ref[0])
    def _():
        _row_gather(src_ref, t * tm, tm, x_hbm, xbuf, sem.at[0])
        x = xbuf[...].astype(bf16)
        g = _dot(x, wg_ref[...].astype(bf16))
        u = _dot(x, wu_ref[...].astype(bf16))
        hh = (g * jax.nn.sigmoid(g)) * u * wt_ref[...]
        y_ref[...] = _dot(hh.astype(bf16), wd_ref[...].astype(bf16))

    @pl.when(t >= nu_ref[0])
    def _():
        y_ref[...] = jnp.zeros_like(y_ref)


def _moe_experts(tile_expert, src_token, n_used, xn, wt_sorted, w_gate, w_up, w_down, tm):
    p_total = src_token.shape[0]
    d = xn.shape[1]
    n_tiles = p_total // tm
    grid_spec = pltpu.PrefetchScalarGridSpec(
        num_scalar_prefetch=3,
        grid=(n_tiles,),
        in_specs=[pl.BlockSpec(memory_space=pl.ANY),
                  pl.BlockSpec((tm, 1), lambda t, te, src, nu: (t, 0)),
                  pl.BlockSpec((None, d, D_EXPERT), lambda t, te, src, nu: (te[t], 0, 0)),
                  pl.BlockSpec((None, d, D_EXPERT), lambda t, te, src, nu: (te[t], 0, 0)),
                  pl.BlockSpec((None, D_EXPERT, d), lambda t, te, src, nu: (te[t], 0, 0))],
        out_specs=pl.BlockSpec((tm, d), lambda t, te, src, nu: (t, 0)),
        scratch_shapes=[pltpu.VMEM((tm, d), f32), pltpu.SemaphoreType.DMA((1,))])
    return pl.pallas_call(
        functools.partial(_moe_kernel, tm=tm),
        grid_spec=grid_spec,
        out_shape=jax.ShapeDtypeStruct((p_total, d), f32),
        compiler_params=_cparams("arbitrary"),
        name="moe_experts",
    )(tile_expert, src_token, n_used, xn, wt_sorted, w_gate, w_up, w_down)


def _combine_kernel(p0_ref, p1_ref, h_ref, g_ref, ys_hbm, o_ref, buf, sems, *, tm):
    i = pl.program_id(0)
    _row_gather(p0_ref, i * tm, tm, ys_hbm, buf.at[0], sems.at[0])
    _row_gather(p1_ref, i * tm, tm, ys_hbm, buf.at[1], sems.at[1])
    h2 = h_ref[...] + (buf[0] + buf[1])
    o_ref[...] = _rms(h2, g_ref[...])


def _combine(pos0, pos1, h1, g_final, y_sorted):
    n, d = h1.shape
    tm = ROW_TILE
    grid_spec = pltpu.PrefetchScalarGridSpec(
        num_scalar_prefetch=2,
        grid=(n // tm,),
        in_specs=[pl.BlockSpec((tm, d), lambda i, a, b: (i, 0)),
                  pl.BlockSpec((1, d), lambda i, a, b: (0, 0)),
                  pl.BlockSpec(memory_space=pl.ANY)],
        out_specs=pl.BlockSpec((tm, d), lambda i, a, b: (i, 0)),
        scratch_shapes=[pltpu.VMEM((2, tm, d), f32), pltpu.SemaphoreType.DMA((2,))])
    return pl.pallas_call(
        functools.partial(_combine_kernel, tm=tm),
        grid_spec=grid_spec,
        out_shape=jax.ShapeDtypeStruct((n, d), f32),
        compiler_params=_cparams("arbitrary"),
        name="moe_combine_norm",
    )(pos0, pos1, h1, g_final.reshape(1, d), y_sorted)


_IN_SIZES = (H_A * DH_A, KVH_A * DH_A, KVH_A * DH_A, H_IDX * D_IDX, H_IDX, D_IDX,
             Q_LORA, KV_LORA, D_ROPE, None, None)


def _pack_w_in(w_in, d):
    sizes = [d if s is None else s for s in _IN_SIZES]
    offs = np.cumsum([0] + sizes)
    qa, ka, va, qi, wi, ki, dq, dkv, krr, ga, gb = [w_in[:, offs[j]:offs[j + 1]] for j in range(11)]
    padw = lambda a, w: jnp.pad(a, ((0, 0), (0, w - a.shape[1])))
    half = D_ROPE // 2
    krs = jnp.concatenate([krr[:, half:], krr[:, :half]], axis=1)
    parts = dict(ga=ga, gb=gb, qa=qa, qi=qi, dq=dq, ka=ka, va=va, dkv=dkv, wi=padw(wi, LANE),
                 ki=padw(ki, LANE), krr=padw(krr, LANE), krs=padw(krs, LANE))
    cols = [parts[n] for n, _ in _SEGS]
    used = sum(w for _, w in _SEGS)
    cols.append(jnp.zeros((w_in.shape[0], PROJ_W - used), w_in.dtype))
    return jnp.concatenate(cols, axis=1).astype(bf16)


def _pack_w_uq(w_uq):
    w = w_uq.reshape(Q_LORA, H_B, D_NOPE + D_ROPE)
    nope = w[:, :, :D_NOPE].reshape(Q_LORA, H_B * D_NOPE)
    rope = w[:, :, D_NOPE:]
    half = D_ROPE // 2
    rope_sw = jnp.concatenate([rope[:, :, half:], rope[:, :, :half]], axis=2)
    padr = lambda a: jnp.pad(a, ((0, 0), (0, 0), (0, LANE - D_ROPE))).reshape(Q_LORA, H_B * LANE)
    return jnp.concatenate([nope, padr(rope), padr(rope_sw)], axis=1).astype(bf16)


def _rope_tables(pos):
    inv = ROPE_THETA ** (-jnp.arange(0, D_ROPE, 2, dtype=f32) / D_ROPE)
    ang = pos.astype(f32)[:, None] * inv[None, :]
    cos, sin = jnp.cos(ang), jnp.sin(ang)
    z = jnp.zeros((pos.shape[0], LANE - D_ROPE), f32)
    return jnp.concatenate([cos, cos, z], axis=1), jnp.concatenate([-sin, sin, z], axis=1)


def _rel_bucket(dist):
    n = jnp.maximum(dist, 0)
    max_exact = N_BUCKETS // 2
    nf = jnp.maximum(n, 1).astype(f32)
    large = max_exact + (jnp.log(nf / max_exact) / math.log(MAX_DIST / max_exact)
                         * (N_BUCKETS - max_exact)).astype(i32)
    large = jnp.minimum(large, N_BUCKETS - 1)
    return jnp.where(n < max_exact, n, large)


def _bias_of_dist(rel_bias, dist):
    b = rel_bias[_rel_bucket(jnp.clip(dist, 0, MAX_DIST))]
    return jnp.moveaxis(b, -1, 0).astype(f32)


def _route(logits):
    pg = jax.nn.softmax(logits[:, :N_GROUPS], axis=-1)
    g_top, g_idx = lax.top_k(pg, 1)
    le = logits[:, N_GROUPS:N_GROUPS + N_EXPERTS].reshape(-1, N_GROUPS, E_PER_GROUP)
    le = jnp.take_along_axis(le, g_idx[:, :, None], axis=1)[:, 0]
    e_top, e_idx = lax.top_k(jax.nn.softmax(le, axis=-1), TOP_E)
    wts = g_top * e_top / jnp.sum(e_top, axis=-1, keepdims=True)
    return g_idx * E_PER_GROUP + e_idx, wts


def _sort_slots(eid, wts, tm):
    n = eid.shape[0]
    a = n * TOP_E
    e_flat = eid.reshape(a)
    onehot = (e_flat[:, None] == jnp.arange(N_EXPERTS, dtype=i32)[None, :]).astype(i32)
    rank = jnp.sum((jnp.cumsum(onehot, axis=0) - onehot) * onehot, axis=1)
    counts = jnp.sum(onehot, axis=0)
    padded = (counts + tm - 1) // tm * tm
    ends = jnp.cumsum(padded)
    starts = ends - padded
    pos = starts[e_flat] + rank
    p_total = (a + tm - 1) // tm * tm + N_EXPERTS * tm
    src_token = jnp.zeros((p_total,), i32).at[pos].set(jnp.arange(a, dtype=i32) // TOP_E)
    wt_sorted = jnp.zeros((p_total,), f32).at[pos].set(wts.reshape(a))
    tile_start = jnp.arange(p_total // tm, dtype=i32) * tm
    tile_expert = jnp.minimum(jnp.searchsorted(ends, tile_start, side="right"), N_EXPERTS - 1).astype(i32)
    n_used = (ends[-1] // tm).astype(i32).reshape(1)
    pos2 = pos.reshape(n, TOP_E)
    return tile_expert, src_token, n_used, wt_sorted.reshape(p_total, 1), pos2[:, 0], pos2[:, 1]


def _pages_per_chunk(n_pages):
    pp = 16
    while n_pages % pp:
        pp //= 2
    return pp


def kernel(x_prompt, x_sample, cache_k, cache_v, cache_idx_k, cache_ckv, cache_kr, page_table,
           meta_tokens, rel_bias, g_attn, w_in, g_q, w_uq, g_kv, w_uk, w_uv, w_oa, w_ob, w_o,
           g_ffn, w_rg, w_re, w_gate, w_up, w_down, g_final):
    nb, s_len, d = x_prompt.shape
    bs, ts, _ = x_sample.shape
    depth = w_in.shape[0]
    assert depth == 1 and ts <= 8
    t_len = s_len + N_META
    tp = -(-t_len // QB) * QB
    n_pages = page_table.shape[1]
    past = n_pages * PAGE
    topk_p = min(TOPK_MAX, s_len // 4)
    topk_s = min(TOPK_MAX, (past + ts) // 4)
    n_prompt = nb * tp
    n_tok = n_prompt + bs * ts
    n_pad = -(-n_tok // ROW_TILE) * ROW_TILE
    if n_pad % 512 == 0:
        tm_proj = 512
    else:
        tm_proj = ROW_TILE

    hp = jnp.concatenate([jnp.broadcast_to(meta_tokens.astype(x_prompt.dtype)[None], (nb, N_META, d)),
                          x_prompt, jnp.zeros((nb, tp - t_len, d), x_prompt.dtype)], axis=1)
    h0 = jnp.concatenate([hp.reshape(n_prompt, d), x_sample.reshape(bs * ts, d),
                          jnp.zeros((n_pad - n_tok, d), x_prompt.dtype)], axis=0)
    pos = jnp.concatenate([jnp.tile(jnp.arange(tp, dtype=i32), nb),
                           jnp.tile(past + jnp.arange(ts, dtype=i32), bs),
                           jnp.zeros((n_pad - n_tok,), i32)])
    cos_t, sin_t = _rope_tables(pos)

    l = 0
    w_pack = _pack_w_in(w_in[l], d)
    wuq_p = _pack_w_uq(w_uq[l])
    wuk_t = jnp.transpose(w_uk[l], (1, 2, 0)).astype(bf16)
    wuv_p = jnp.transpose(w_uv[l], (1, 0, 2)).astype(bf16)
    wr = jnp.pad(jnp.concatenate([w_rg[l], w_re[l]], axis=1),
                 ((0, 0), (0, LANE - N_GROUPS - N_EXPERTS))).astype(bf16)

    proj = _norm_matmul(h0, g_attn[l], w_pack, tm_proj, 1024)
    qcat, ckv, kr, kcat, kab, vab, kib = _post_project(proj, cos_t, sin_t, g_q[l], g_kv[l], wuq_p, wuk_t)

    tq = jnp.arange(QB, dtype=i32)[:, None]
    jj = jnp.arange(BAND, dtype=i32)[None, :]
    band = _bias_of_dist(rel_bias, tq + QB - jj)
    far = _bias_of_dist(rel_bias, jnp.full((1, LANE), MAX_DIST, i32))
    oa_p = _dsa_prompt(proj, kib, kab, vab, band, far, nb, tp, topk_p)
    ob_p = _mla_prompt(qcat, kcat, nb, tp)

    pp = _pages_per_chunk(n_pages)
    ch = pp * PAGE
    srow = lambda a: a[n_prompt:n_prompt + bs * ts]
    j_qi, j_wi, j_qa = _COL["qi"], _COL["wi"], _COL["qa"]
    seg = lambda name: srow(proj)[:, _COL[name][0] * _COL[name][1]:(_COL[name][0] + 1) * _COL[name][1]]
    pad8 = lambda a: jnp.pad(a, ((0, 0), (0, 8 - ts)) + ((0, 0),) * (a.ndim - 2))
    qi_s = pad8(seg("qi").reshape(bs, ts, H_IDX, D_IDX)).transpose(0, 2, 1, 3).reshape(bs, H_IDX * 8, D_IDX)
    wi_s = pad8(seg("wi")[:, :H_IDX].reshape(bs, ts, H_IDX)).transpose(0, 2, 1).reshape(bs, H_IDX * 8, 1)
    nblk_s = -(-(bs * ts) // QB)
    qc_s = qcat[n_prompt // QB:n_prompt // QB + nblk_s].transpose(0, 2, 1, 3)
    qc_s = qc_s.reshape(nblk_s * QB, H_B, MLA_KCAT)[:bs * ts]
    qc_s = pad8(qc_s.reshape(bs, ts, H_B, MLA_KCAT)).transpose(0, 2, 1, 3).reshape(bs, H_B * 8, MLA_KCAT)
    qa_s = pad8(seg("qa").reshape(bs, ts, KVH_A, REP_A, DH_A)).transpose(0, 2, 3, 1, 4)
    qa_s = qa_s.reshape(bs, KVH_A, REP_A * 8, DH_A).astype(bf16)
    padp = lambda a: jnp.pad(a.reshape(bs, ts, a.shape[-1]), ((0, 0), (0, PAGE - ts), (0, 0)))
    ki_new = padp(srow(kib)[:, :D_IDX])
    kc_new = padp(srow(kcat))
    ka_new = padp(srow(kab))
    va_new = padp(srow(vab))
    scores, ob_s = _sample_pass_a(page_table, qi_s.astype(bf16), wi_s * IDX_W_SCALE, qc_s, ki_new, kc_new,
                                  cache_idx_k, cache_ckv, cache_kr, ts, pp)
    j8 = jnp.arange(8, dtype=i32)[:, None]
    bias_last = _bias_of_dist(rel_bias, j8 + ch - jnp.arange(ch, dtype=i32)[None, :])
    bias_new = _bias_of_dist(rel_bias, j8 - jnp.arange(PAGE, dtype=i32)[None, :])
    shp = lambda a: a.reshape((KVH_A, REP_A) + a.shape[1:])
    oa_s = _sample_pass_b(page_table, scores, qa_s, ka_new, va_new, shp(bias_last), shp(bias_new),
                          shp(far[:, :, :1]), cache_k, cache_v, topk_s, pp)
    oa_s = oa_s.reshape(bs, KVH_A, REP_A, 8, DH_A)[:, :, :, :ts].transpose(0, 3, 1, 2, 4)
    oa_s = oa_s.reshape(bs * ts, H_A * DH_A).astype(bf16)
    ob_s = ob_s.reshape(bs, H_B, 8, KV_LORA)[:, :, :ts].transpose(0, 2, 1, 3)
    ob_s = ob_s.reshape(bs * ts, H_B * KV_LORA).astype(bf16)
    tail = lambda w: jnp.zeros((n_pad - n_tok, w), bf16)
    oa = jnp.concatenate([oa_p, oa_s, tail(H_A * DH_A)], axis=0)
    ob = jnp.concatenate([ob_p, ob_s, tail(H_B * KV_LORA)], axis=0)

    h1, xn, logits = _merge(oa, ob, proj, h0, w_oa[l].astype(bf16), wuv_p, w_ob[l].astype(bf16),
                            w_o[l].astype(bf16), g_ffn[l], wr)
    eid, wts = _route(logits)
    tile_expert, src_token, n_used, wt_sorted, pos0, pos1 = _sort_slots(eid, wts, ROW_TILE)
    y_sorted = _moe_experts(tile_expert, src_token, n_used, xn, wt_sorted, w_gate[l], w_up[l], w_down[l],
                            ROW_TILE)
    y = _combine(pos0, pos1, h1, g_final, y_sorted)

    y_prompt = y[:n_prompt].reshape(nb, tp, d)[:, N_META:t_len]
    y_sample = y[n_prompt:n_tok].reshape(bs, ts, d)

    def states(rows, lead):
        ka = rows(seg_all("ka")).reshape(lead + (KVH_A, DH_A))
        va = rows(seg_all("va")).reshape(lead + (KVH_A, DH_A))
        ki = rows(seg_all("ki"))[..., :D_IDX].reshape(lead + (D_IDX,))
        cc = rows(ckv).reshape(lead + (KV_LORA,))
        rr = rows(kr)[..., :D_ROPE].reshape(lead + (D_ROPE,))
        return [a[None] for a in (ka, va, ki, cc, rr)]

    seg_all = lambda name: proj[:, _COL[name][0] * _COL[name][1]:(_COL[name][0] + 1) * _COL[name][1]]
    prow = lambda a: a[:n_prompt].reshape(nb, tp, a.shape[-1])[:, :t_len]
    st_p = states(prow, (nb, t_len))
    st_s = states(lambda a: a[n_prompt:n_tok], (bs, ts))
    return (y_prompt, y_sample, *st_p, *st_s)
```

```python
import functools
import math

import numpy as np
import jax
import jax.numpy as jnp
from jax import lax
from jax.experimental import pallas as pl
from jax.experimental.pallas import tpu as pltpu

N_META = 16
H_A, KVH_A, DH_A = 8, 2, 128
REP_A = H_A // KVH_A
H_IDX, D_IDX = 16, 64
TOPK_MAX = 256
IDX_W_SCALE = (H_IDX ** -0.5) * (D_IDX ** -0.5)
N_BUCKETS, MAX_DIST = 32, 128
H_B, Q_LORA, KV_LORA, D_NOPE, D_ROPE, D_V = 8, 512, 256, 128, 64, 128
ROPE_THETA = 10000.0
MLA_SCALE = (D_NOPE + D_ROPE) ** -0.5
DSA_SCALE = DH_A ** -0.5
N_GROUPS, E_PER_GROUP, TOP_E, D_EXPERT = 4, 8, 2, 512
N_EXPERTS = N_GROUPS * E_PER_GROUP
EPS = 1e-6
PAGE = 128

LANE = 128
SUBLANE = 8
VMEM_LIMIT_BYTES = 56 * 1024 * 1024

QB = 128
ROW_TILE = 256
MLA_KCAT = KV_LORA + LANE
BAND = 2 * QB
INT_MIN = -2 ** 31
NEG_INF_KEY = -2 ** 31 + 0x7FFFFF

_SEGS = (("ga", 2048), ("gb", 2048), ("qa", 1024), ("qi", 1024), ("dq", 512), ("ka", 256),
         ("va", 256), ("dkv", 256), ("wi", 128), ("ki", 128), ("krr", 128), ("krs", 128))
_COL = {}
_o = 0
for _n, _w in _SEGS:
    assert _o % _w == 0
    _COL[_n] = (_o // _w, _w)
    _o += _w
PROJ_W = 8192
assert _o <= PROJ_W

f32, bf16, i32 = jnp.float32, jnp.bfloat16, jnp.int32


def _cparams(*sem):
    return pltpu.CompilerParams(dimension_semantics=sem, vmem_limit_bytes=VMEM_LIMIT_BYTES)


def _dot(a, b):
    return jnp.dot(a, b, preferred_element_type=f32)


def _dot_nt(a, b):
    return lax.dot_general(a, b, (((1,), (1,)), ((), ())), preferred_element_type=f32)


def _rms(x, g):
    return x * lax.rsqrt(jnp.mean(x * x, axis=-1, keepdims=True) + EPS) * g


def _const_spec(shape):
    nd = len(shape)
    return pl.BlockSpec(shape, lambda *_: (0,) * nd, pipeline_mode=pl.Buffered(1))


def _split_rows(x):
    return [x[:, j * LANE:(j + 1) * LANE] for j in range(x.shape[1] // LANE)]


def _norm_matmul_kernel(x_ref, g_ref, w_ref, o_ref, xn_ref):
    @pl.when(pl.program_id(1) == 0)
    def _():
        xn_ref[...] = _rms(x_ref[...], g_ref[...]).astype(bf16)

    o_ref[...] = _dot(xn_ref[...], w_ref[...])


def _norm_matmul(x, g, w, tm, tn):
    n, k = x.shape
    m = w.shape[1]
    return pl.pallas_call(
        _norm_matmul_kernel,
        grid=(n // tm, m // tn),
        in_specs=[pl.BlockSpec((tm, k), lambda i, j: (i, 0)),
                  pl.BlockSpec((1, k), lambda i, j: (0, 0)),
                  pl.BlockSpec((k, tn), lambda i, j: (0, j))],
        out_specs=pl.BlockSpec((tm, tn), lambda i, j: (i, j)),
        out_shape=jax.ShapeDtypeStruct((n, m), f32),
        scratch_shapes=[pltpu.VMEM((tm, k), bf16)],
        compiler_params=_cparams("parallel", "arbitrary"),
        name="in_proj",
    )(x, g.reshape(1, k), w)


def _post_kernel(dq_ref, dkv_ref, krr_ref, krs_ref, ka_ref, va_ref, ki_ref, cos_ref, sin_ref,
                 gq_ref, gkv_ref, wuq_ref, wuk_ref,
                 qcat_ref, ckv_ref, kr_ref, kcat_ref, kab_ref, vab_ref, kib_ref):
    tm = dq_ref.shape[0]
    cos_t = cos_ref[...]
    sin_t = sin_ref[...]
    dqn = _rms(dq_ref[...], gq_ref[...]).astype(bf16)
    q = _dot(dqn, wuq_ref[...])
    for h in range(H_B):
        qlat = _dot(q[:, h * 128:(h + 1) * 128].astype(bf16), wuk_ref[h])
        qr = (q[:, 1024 + h * 128:1024 + (h + 1) * 128] * cos_t
              + q[:, 2048 + h * 128:2048 + (h + 1) * 128] * sin_t)
        for sb in range(tm // QB):
            qcat_ref[sb, h, :, 0:KV_LORA] = qlat[sb * QB:(sb + 1) * QB].astype(bf16)
            qcat_ref[sb, h, :, KV_LORA:MLA_KCAT] = qr[sb * QB:(sb + 1) * QB].astype(bf16)
    ckv = _rms(dkv_ref[...], gkv_ref[...])
    ckv_ref[...] = ckv
    kr = krr_ref[...] * cos_t + krs_ref[...] * sin_t
    kr_ref[...] = kr
    kcat_ref[:, 0:KV_LORA] = ckv.astype(bf16)
    kcat_ref[:, KV_LORA:MLA_KCAT] = kr.astype(bf16)
    kab_ref[...] = ka_ref[...].astype(bf16)
    vab_ref[...] = va_ref[...].astype(bf16)
    kib_ref[...] = ki_ref[...].astype(bf16)


def _post_project(proj, cos_t, sin_t, g_q, g_kv, wuq_p, wuk_t):
    n = proj.shape[0]
    tm = ROW_TILE

    def col(name):
        j, w = _COL[name]
        return pl.BlockSpec((tm, w), lambda i, j=j: (i, j))

    row = lambda w: pl.BlockSpec((tm, w), lambda i: (i, 0))
    outs = pl.pallas_call(
        _post_kernel,
        grid=(n // tm,),
        in_specs=[col("dq"), col("dkv"), col("krr"), col("krs"), col("ka"), col("va"), col("ki"),
                  row(LANE), row(LANE),
                  _const_spec((1, Q_LORA)), _const_spec((1, KV_LORA)),
                  _const_spec(wuq_p.shape), _const_spec(wuk_t.shape)],
        out_specs=[pl.BlockSpec((tm // QB, H_B, QB, MLA_KCAT), lambda i: (i, 0, 0, 0)),
                   row(KV_LORA), row(LANE), row(MLA_KCAT), row(256), row(256), row(LANE)],
        out_shape=[jax.ShapeDtypeStruct((n // QB, H_B, QB, MLA_KCAT), bf16),
                   jax.ShapeDtypeStruct((n, KV_LORA), f32),
                   jax.ShapeDtypeStruct((n, LANE), f32),
                   jax.ShapeDtypeStruct((n, MLA_KCAT), bf16),
                   jax.ShapeDtypeStruct((n, 256), bf16),
                   jax.ShapeDtypeStruct((n, 256), bf16),
                   jax.ShapeDtypeStruct((n, LANE), bf16)],
        compiler_params=_cparams("parallel"),
        name="post_proj",
    )(proj, proj, proj, proj, proj, proj, proj, cos_t, sin_t,
      g_q.reshape(1, -1), g_kv.reshape(1, -1), wuq_p, wuk_t)
    return outs


def _mla_prompt_kernel(q_ref, k_ref, o_ref, *, tk):
    i = pl.program_id(1)
    tp = k_ref.shape[0]
    t0 = i * QB
    rows = H_B * QB
    q = q_ref[0].reshape(rows, MLA_KCAT)
    qpos = t0 + (lax.broadcasted_iota(i32, (rows, 1), 0) & (QB - 1))
    nk = (t0 + QB + tk - 1) // tk

    def body(c, carry):
        m, l, acc = carry
        lo = c * tk
        st = pl.multiple_of(jnp.minimum(lo, tp - tk), QB)
        ks = k_ref[pl.ds(st, tk), :]
        s = _dot_nt(q, ks) * MLA_SCALE
        kpos = st + lax.broadcasted_iota(i32, (1, tk), 1)
        s = jnp.where((kpos <= qpos) & (kpos >= lo), s, -jnp.inf)
        m_new = jnp.maximum(m, jnp.max(s, axis=1, keepdims=True))
        p = jnp.exp(s - m_new)
        alpha = jnp.exp(m - m_new)
        l = alpha * l + jnp.sum(p, axis=1, keepdims=True)
        acc = alpha * acc + _dot(p.astype(bf16), ks[:, 0:KV_LORA])
        return m_new, l, acc

    m0 = jnp.full((rows, 1), -jnp.inf, f32)
    l0 = jnp.zeros((rows, 1), f32)
    a0 = jnp.zeros((rows, KV_LORA), f32)
    _, l, acc = lax.fori_loop(0, nk, body, (m0, l0, a0))
    o = acc / l
    for h in range(H_B):
        o_ref[:, h * KV_LORA:(h + 1) * KV_LORA] = o[h * QB:(h + 1) * QB].astype(bf16)


def _mla_prompt(qcat, kcat, nb, tp):
    nq = tp // QB
    tk = min(512, tp)
    return pl.pallas_call(
        functools.partial(_mla_prompt_kernel, tk=tk),
        grid=(nb, nq),
        in_specs=[pl.BlockSpec((1, H_B, QB, MLA_KCAT), lambda b, i: (b * nq + i, 0, 0, 0)),
                  pl.BlockSpec((tp, MLA_KCAT), lambda b, i: (b, 0))],
        out_specs=pl.BlockSpec((QB, H_B * KV_LORA), lambda b, i: (b * nq + i, 0)),
        out_shape=jax.ShapeDtypeStruct((nb * tp, H_B * KV_LORA), bf16),
        compiler_params=_cparams("parallel", "arbitrary"),
        name="mla_prompt",
    )(qcat, kcat)


def _sortable(score):
    b = lax.bitcast_convert_type(score, i32)
    return jnp.where(b < 0, b ^ jnp.int32(0x7FFFFFFF), b)


def _kth_largest(key_ref, k, count_axes, shape1):
    def body(it, res_u):
        cand_u = res_u | lax.shift_left(jnp.int32(1), 31 - it)
        cand_s = cand_u ^ jnp.int32(INT_MIN)
        ge = (key_ref[...] >= cand_s).astype(i32)
        cnt = jnp.sum(ge, axis=count_axes, keepdims=True)
        return jnp.where(cnt >= k, cand_u, res_u)

    res_u = lax.fori_loop(0, 32, body, jnp.zeros(shape1, i32), unroll=4)
    return res_u ^ jnp.int32(INT_MIN)


def _online_update(carry, s, v):
    m, l, acc = carry
    m_new = jnp.maximum(m, jnp.max(s, axis=1, keepdims=True))
    m_safe = jnp.where(m_new == -jnp.inf, 0.0, m_new)
    p = jnp.exp(s - m_safe)
    alpha = jnp.exp(m - m_safe)
    l = alpha * l + jnp.sum(p, axis=1, keepdims=True)
    acc = alpha * acc + _dot(p.astype(bf16), v)
    return m_new, l, acc


def _dsa_prompt_kernel(qa_ref, qi_ref, wi_ref, ki_ref, ka_ref, va_ref, band_ref, far_ref, o_ref,
                       key_ref, qs_ref, qg_ref, m_ref, l_ref, acc_ref, *, topk):
    i = pl.program_id(1)
    t0 = i * QB
    tp = ki_ref.shape[0]
    nck = tp // QB
    wide = 2 * QB
    qpos = t0 + lax.broadcasted_iota(i32, (QB, 1), 0)

    qi = qi_ref[...]
    for h in range(H_IDX):
        qs_ref[h * QB:(h + 1) * QB, :] = qi[:, h * D_IDX:(h + 1) * D_IDX].astype(bf16)
    qa = qa_ref[...]
    for h in range(H_A):
        qg_ref[h * QB:(h + 1) * QB, :] = qa[:, h * DH_A:(h + 1) * DH_A].astype(bf16)
    wi = wi_ref[...] * IDX_W_SCALE

    def score_chunk(c, carry):
        st = pl.multiple_of(jnp.minimum(c * wide, tp - wide), QB)
        s_all = _dot_nt(qs_ref[...], ki_ref[pl.ds(st, wide), 0:D_IDX])
        score = jnp.zeros((QB, wide), f32)
        for h in range(H_IDX):
            score = score + wi[:, h:h + 1] * jnp.maximum(s_all[h * QB:(h + 1) * QB], 0.0)
        kpos = st + lax.broadcasted_iota(i32, (1, wide), 1)
        key = jnp.where(kpos <= qpos, _sortable(score), jnp.int32(NEG_INF_KEY))
        cj = st // QB
        key_ref[cj] = key[:, 0:QB]
        key_ref[cj + 1] = key[:, QB:wide]
        return carry

    lax.fori_loop(0, (i + 2) // 2, score_chunk, 0)

    def blank_chunk(c, carry):
        key_ref[c] = jnp.full((QB, QB), NEG_INF_KEY, i32)
        return carry

    lax.fori_loop(i + 1, nck, blank_chunk, 0)

    thr = _kth_largest(key_ref, topk, (0, 2), (1, QB, 1))[0]

    rows = H_A * QB
    m_ref[...] = jnp.full((rows, 1), -jnp.inf, f32)
    l_ref[...] = jnp.zeros((rows, 1), f32)
    acc_ref[...] = jnp.zeros((rows, DH_A), f32)

    def attend(st, width, sel, bias_of):
        for g in range(KVH_A):
            rs = slice(g * REP_A * QB, (g + 1) * REP_A * QB)
            lg = _dot_nt(qg_ref[rs, :], ka_ref[pl.ds(st, width), g * DH_A:(g + 1) * DH_A]) * DSA_SCALE
            x = jnp.concatenate(
                [jnp.where(sel, lg[r * QB:(r + 1) * QB] + bias_of(REP_A * g + r), -jnp.inf)
                 for r in range(REP_A)], axis=0)
            m, l, acc = _online_update((m_ref[rs, :], l_ref[rs, :], acc_ref[rs, :]), x,
                                       va_ref[pl.ds(st, width), g * DH_A:(g + 1) * DH_A])
            m_ref[rs, :] = m
            l_ref[rs, :] = l
            acc_ref[rs, :] = acc

    def far_chunk(c, carry):
        st = pl.multiple_of(c * wide, wide)
        kk = key_ref[pl.ds(2 * c, 2)]
        key = jnp.concatenate([kk[0], kk[1]], axis=1)
        kpos = st + lax.broadcasted_iota(i32, (1, wide), 1)
        attend(st, wide, (key >= thr) & (kpos < t0 - QB), lambda h: far_ref[h][:, 0:1])
        return carry

    lax.fori_loop(0, i // 2, far_chunk, 0)

    @pl.when(i > 0)
    def _():
        attend(pl.multiple_of(t0 - QB, QB), QB, key_ref[jnp.maximum(i - 1, 0)] >= thr,
               lambda h: band_ref[h][:, 0:QB])

    diag = (t0 + lax.broadcasted_iota(i32, (1, QB), 1)) <= qpos
    attend(pl.multiple_of(t0, QB), QB, (key_ref[i] >= thr) & diag, lambda h: band_ref[h][:, QB:BAND])

    o = acc_ref[...] / l_ref[...]
    for h in range(H_A):
        o_ref[:, h * DH_A:(h + 1) * DH_A] = o[h * QB:(h + 1) * QB].astype(bf16)


def _dsa_prompt(proj, kib, kab, vab, band, far, nb, tp, topk):
    nq = tp // QB
    jqa, jqi, jwi = _COL["qa"][0], _COL["qi"][0], _COL["wi"][0]
    return pl.pallas_call(
        functools.partial(_dsa_prompt_kernel, topk=topk),
        grid=(nb, nq),
        in_specs=[pl.BlockSpec((QB, 1024), lambda b, i: (b * nq + i, jqa)),
                  pl.BlockSpec((QB, 1024), lambda b, i: (b * nq + i, jqi)),
                  pl.BlockSpec((QB, LANE), lambda b, i: (b * nq + i, jwi)),
                  pl.BlockSpec((tp, LANE), lambda b, i: (b, 0)),
                  pl.BlockSpec((tp, 256), lambda b, i: (b, 0)),
                  pl.BlockSpec((tp, 256), lambda b, i: (b, 0)),
                  _const_spec((H_A, QB, BAND)),
                  _const_spec((H_A, 1, LANE))],
        out_specs=pl.BlockSpec((QB, H_A * DH_A), lambda b, i: (b * nq + i, 0)),
        out_shape=jax.ShapeDtypeStruct((nb * tp, H_A * DH_A), bf16),
        scratch_shapes=[pltpu.VMEM((tp // QB, QB, QB), i32),
                        pltpu.VMEM((H_IDX * QB, D_IDX), bf16),
                        pltpu.VMEM((H_A * QB, DH_A), bf16),
                        pltpu.VMEM((H_A * QB, 1), f32), pltpu.VMEM((H_A * QB, 1), f32),
                        pltpu.VMEM((H_A * QB, DH_A), f32)],
        compiler_params=_cparams("parallel", "arbitrary"),
        name="dsa_prompt",
    )(proj, proj, proj, kib, kab, vab, band, far)


def _page_copies(pt_ref, base, c, slot, pp, srcs, bufs, sems):
    out = []
    for p in range(pp):
        page = pt_ref[base + c * pp + p]
        for a, (src, buf) in enumerate(zip(srcs, bufs)):
            out.append(pltpu.make_async_copy(src(page), buf(slot, p), sems.at[slot, a]))
    return out


def _sample_a_kernel(pt_ref, qi_ref, wi_ref, qc_ref, kin_ref, kcn_ref, idx_hbm, ckv_hbm, kr_hbm,
                     sc_ref, ob_ref, ibuf, cbuf, rbuf, sems, *, n_pages, pp, ts, rs):
    s_id = pl.program_id(0)
    pack = SUBLANE // rs
    par = s_id % pack
    n_chunks = n_pages // pp
    ch = pp * PAGE
    base = s_id * n_pages
    srcs = (lambda pg: idx_hbm.at[0, pg], lambda pg: ckv_hbm.at[0, pg], lambda pg: kr_hbm.at[0, pg])
    bufs = (lambda sl, p: ibuf.at[sl, :, pl.ds(p * PAGE, PAGE)],
            lambda sl, p: cbuf.at[sl, pl.ds(p * PAGE, PAGE)],
            lambda sl, p: rbuf.at[sl, :, pl.ds(p * PAGE, PAGE)])

    def start(c, slot):
        for cp in _page_copies(pt_ref, base, c, slot, pp, srcs, bufs, sems):
            cp.start()

    def wait(c, slot):
        for cp in _page_copies(pt_ref, base, c, slot, pp, srcs, bufs, sems):
            cp.wait()

    qi = qi_ref[0].astype(bf16)
    w = wi_ref[0]
    qlat = qc_ref[0][:, 0:KV_LORA]
    qrope = qc_ref[0][:, KV_LORA:KV_LORA + D_ROPE]
    rows = H_B * SUBLANE

    def head_sum(s):
        s = jnp.maximum(s, 0.0) * w
        return jnp.sum(s.reshape(H_IDX, SUBLANE, s.shape[1]), axis=0)

    def put_scores(c, sc):
        for q in range(pack):
            @pl.when(par == q)
            def _():
                sc_ref[0, c, q * rs:(q + 1) * rs, :] = sc[q * rs:(q + 1) * rs]

    start(0, 0)

    def body(c, carry):
        slot = c % 2

        @pl.when(c + 1 < n_chunks)
        def _():
            start(c + 1, 1 - slot)

        wait(c, slot)
        put_scores(c, head_sum(_dot(qi, ibuf[slot].astype(bf16))))
        ck = cbuf[slot].astype(bf16)
        s = (_dot_nt(qlat, ck) + _dot(qrope, rbuf[slot].astype(bf16))) * MLA_SCALE
        return _online_update(carry, s, ck)

    carry = (jnp.full((rows, 1), -jnp.inf, f32), jnp.zeros((rows, 1), f32),
             jnp.zeros((rows, KV_LORA), f32))
    carry = lax.fori_loop(0, n_chunks, body, carry)

    kj = lax.broadcasted_iota(i32, (1, PAGE), 1)
    ok8 = (kj <= (lax.broadcasted_iota(i32, (SUBLANE, 1), 0) & (rs - 1))) & (kj < ts)
    sc_new = jnp.where(ok8, head_sum(_dot_nt(qi, kin_ref[0])), -jnp.inf)
    if ch > PAGE:
        sc_new = jnp.concatenate([sc_new, jnp.full((SUBLANE, ch - PAGE), -jnp.inf, f32)], axis=1)
    put_scores(n_chunks, sc_new)
    kcn = kcn_ref[0]
    okr = (kj <= (lax.broadcasted_iota(i32, (rows, 1), 0) & (SUBLANE - 1))) & (kj < ts)
    s = jnp.where(okr, _dot_nt(qc_ref[0], kcn) * MLA_SCALE, -jnp.inf)
    _, l, acc = _online_update(carry, s, kcn[:, 0:KV_LORA])
    ob_ref[0] = acc / l


def _sample_pass_a(page_table, qi_s, wi_s, qc_s, ki_new, kc_new, idx_t, cache_ckv, kr_t, ts, rs, pp):
    bs, n_pages = page_table.shape
    pack = SUBLANE // rs
    n_chunks = n_pages // pp
    ch = pp * PAGE
    blk = lambda *shape: pl.BlockSpec((1,) + shape, lambda s, pt: (s,) + (0,) * len(shape))
    pblk = lambda *shape: pl.BlockSpec((1,) + shape, lambda s, pt: (s // pack,) + (0,) * len(shape))
    any_spec = pl.BlockSpec(memory_space=pl.ANY)
    grid_spec = pltpu.PrefetchScalarGridSpec(
        num_scalar_prefetch=1,
        grid=(bs,),
        in_specs=[pblk(H_IDX * SUBLANE, D_IDX), pblk(H_IDX * SUBLANE, 1), blk(H_B * SUBLANE, MLA_KCAT),
                  blk(PAGE, D_IDX), blk(PAGE, MLA_KCAT), any_spec, any_spec, any_spec],
        out_specs=[pblk(n_chunks + 1, SUBLANE, ch), blk(H_B * SUBLANE, KV_LORA)],
        scratch_shapes=[pltpu.VMEM((2, D_IDX, ch), f32), pltpu.VMEM((2, ch, KV_LORA), f32),
                        pltpu.VMEM((2, D_ROPE, ch), f32), pltpu.SemaphoreType.DMA((2, 3))])
    return pl.pallas_call(
        functools.partial(_sample_a_kernel, n_pages=n_pages, pp=pp, ts=ts, rs=rs),
        grid_spec=grid_spec,
        out_shape=[jax.ShapeDtypeStruct((bs // pack, n_chunks + 1, SUBLANE, ch), f32),
                   jax.ShapeDtypeStruct((bs, H_B * SUBLANE, KV_LORA), f32)],
        compiler_params=_cparams("arbitrary"),
        name="sample_idx_mla",
    )(page_table.reshape(-1), qi_s, wi_s, qc_s, ki_new, kc_new, idx_t, cache_ckv, kr_t)


def _thr_kernel(sc_ref, thr_ref, key_ref, *, topk):
    g = sc_ref.shape[0]
    key_ref[...] = _sortable(sc_ref[...])
    thr = _kth_largest(key_ref, topk, (1, 3), (g, 1, SUBLANE, 1))
    thr_ref[...] = jnp.broadcast_to(thr[:, 0], (g, SUBLANE, LANE))


def _sample_thresholds(scores, topk):
    nblk, nc1, _, ch = scores.shape
    g = 8
    while nblk % g:
        g //= 2
    return pl.pallas_call(
        functools.partial(_thr_kernel, topk=topk),
        grid=(nblk // g,),
        in_specs=[pl.BlockSpec((g, nc1, SUBLANE, ch), lambda i: (i, 0, 0, 0))],
        out_specs=pl.BlockSpec((g, SUBLANE, LANE), lambda i: (i, 0, 0)),
        out_shape=jax.ShapeDtypeStruct((nblk, SUBLANE, LANE), i32),
        scratch_shapes=[pltpu.VMEM((g, nc1, SUBLANE, ch), i32)],
        compiler_params=_cparams("parallel"),
        name="sample_topk_thr",
    )(scores)


def _sample_b_kernel(pt_ref, sc_ref, thr_ref, qa_ref, kn_ref, vn_ref, bl_ref, bn_ref, far_ref, k_hbm, v_hbm,
                     o_ref, kbuf, vbuf, sems, *, n_pages, pp):
    s_id = pl.program_id(0)
    n_chunks = n_pages // pp
    base = s_id * n_pages
    srcs, bufs = [], []
    for g in range(KVH_A):
        srcs.append(lambda pg, g=g: k_hbm.at[0, pg, :, g, :])
        bufs.append(lambda sl, p, g=g: kbuf.at[sl, g, pl.ds(p * PAGE, PAGE)])
        srcs.append(lambda pg, g=g: v_hbm.at[0, pg, :, g, :])
        bufs.append(lambda sl, p, g=g: vbuf.at[sl, g, pl.ds(p * PAGE, PAGE)])

    def start(c, slot):
        for cp in _page_copies(pt_ref, base, c, slot, pp, srcs, bufs, sems):
            cp.start()

    def wait(c, slot):
        for cp in _page_copies(pt_ref, base, c, slot, pp, srcs, bufs, sems):
            cp.wait()

    start(0, 0)
    thr = thr_ref[0][:, 0:1]
    rows = REP_A * SUBLANE
    qa = [qa_ref[0, g].astype(bf16) for g in range(KVH_A)]

    def attend(carry, sc, kk, vv, bias_of):
        sel = _sortable(sc) >= thr
        n = sel.shape[1]
        new = []
        for g in range(KVH_A):
            lg = _dot_nt(qa[g], kk(g)) * DSA_SCALE
            lg = lg.reshape(REP_A, SUBLANE, n) + bias_of(g)
            lg = jnp.where(sel[None], lg, -jnp.inf).reshape(rows, n)
            new.append(_online_update(carry[g], lg, vv(g)))
        return tuple(new)

    def body(c, carry):
        slot = c % 2

        @pl.when(c + 1 < n_chunks)
        def _():
            start(c + 1, 1 - slot)

        wait(c, slot)
        return attend(carry, sc_ref[0, c], lambda g: kbuf[slot, g].astype(bf16),
                      lambda g: vbuf[slot, g].astype(bf16), lambda g: far_ref[g])

    init = (jnp.full((rows, 1), -jnp.inf, f32), jnp.zeros((rows, 1), f32), jnp.zeros((rows, DH_A), f32))
    carry = lax.fori_loop(0, n_chunks - 1, body, (init,) * KVH_A)
    last = n_chunks - 1
    slot = last % 2
    wait(last, slot)
    carry = attend(carry, sc_ref[0, last], lambda g: kbuf[slot, g].astype(bf16),
                   lambda g: vbuf[slot, g].astype(bf16), lambda g: bl_ref[g])
    sel_new = _sortable(sc_ref[0, n_chunks][:, 0:PAGE]) >= thr
    for g in range(KVH_A):
        lg = _dot_nt(qa[g], kn_ref[0][:, g * DH_A:(g + 1) * DH_A]) * DSA_SCALE
        lg = lg.reshape(REP_A, SUBLANE, PAGE) + bn_ref[g]
        lg = jnp.where(sel_new[None], lg, -jnp.inf).reshape(rows, PAGE)
        _, l, acc = _online_update(carry[g], lg, vn_ref[0][:, g * DH_A:(g + 1) * DH_A])
        o_ref[0, g] = acc / l


def _sample_pass_b(page_table, scores, thr, qa_s, ka_new, va_new, bias_last, bias_new, far_s, cache_k, cache_v,
                   rs, pp):
    bs, n_pages = page_table.shape
    pack = SUBLANE // rs
    n_chunks = n_pages // pp
    ch = pp * PAGE
    blk = lambda *shape: pl.BlockSpec((1,) + shape, lambda s, pt: (s,) + (0,) * len(shape))
    pblk = lambda *shape: pl.BlockSpec((1,) + shape, lambda s, pt: (s // pack,) + (0,) * len(shape))
    cst = lambda shape: pl.BlockSpec(shape, lambda s, pt: (0,) * len(shape))
    any_spec = pl.BlockSpec(memory_space=pl.ANY)
    grid_spec = pltpu.PrefetchScalarGridSpec(
        num_scalar_prefetch=1,
        grid=(bs,),
        in_specs=[pblk(n_chunks + 1, SUBLANE, ch), pblk(SUBLANE, LANE), pblk(KVH_A, REP_A * SUBLANE, DH_A),
                  blk(PAGE, 256), blk(PAGE, 256),
                  cst((KVH_A, REP_A, SUBLANE, ch)), cst((KVH_A, REP_A, SUBLANE, PAGE)), cst((KVH_A, REP_A, 1, 1)),
                  any_spec, any_spec],
        out_specs=[blk(KVH_A, REP_A * SUBLANE, DH_A)],
        scratch_shapes=[pltpu.VMEM((2, KVH_A, ch, DH_A), f32), pltpu.VMEM((2, KVH_A, ch, DH_A), f32),
                        pltpu.SemaphoreType.DMA((2, 2 * KVH_A))])
    return pl.pallas_call(
        functools.partial(_sample_b_kernel, n_pages=n_pages, pp=pp),
        grid_spec=grid_spec,
        out_shape=[jax.ShapeDtypeStruct((bs, KVH_A, REP_A * SUBLANE, DH_A), f32)],
        compiler_params=_cparams("arbitrary"),
        name="sample_dsa",
    )(page_table.reshape(-1), scores, thr, qa_s, ka_new, va_new, bias_last, bias_new, far_s, cache_k, cache_v)[0]


def _route_tile(lg):
    lane = lax.broadcasted_iota(i32, lg.shape, 1)
    lane_f = lane.astype(f32)
    first = lambda hit: jnp.min(jnp.where(hit, lane_f, float(LANE)), axis=1, keepdims=True)
    gl = jnp.where(lane < N_GROUPS, lg, -jnp.inf)
    gm = jnp.max(gl, axis=1, keepdims=True)
    g_top = 1.0 / jnp.sum(jnp.exp(gl - gm), axis=1, keepdims=True)
    lo = N_GROUPS + E_PER_GROUP * first(gl == gm)
    el = jnp.where((lane_f >= lo) & (lane_f < lo + E_PER_GROUP), lg, -jnp.inf)
    em1 = jnp.max(el, axis=1, keepdims=True)
    i1 = first(el == em1)
    el2 = jnp.where(lane_f == i1, -jnp.inf, el)
    em2 = jnp.max(el2, axis=1, keepdims=True)
    i2 = first(el2 == em2)
    r = jnp.exp(em2 - em1)
    w1 = g_top / (1.0 + r)
    w2 = g_top * r / (1.0 + r)
    out = jnp.where(lane == 0, i1 - N_GROUPS, 0.0)
    out = jnp.where(lane == 1, i2 - N_GROUPS, out)
    out = jnp.where(lane == 2, w1, out)
    return jnp.where(lane == 3, w2, out)


def _merge_kernel(oa_ref, ob_ref, ga_ref, gb_ref, h_ref, woa_ref, wuv_ref, wob_ref, wo_ref, gf_ref, wr_ref,
                  h1_ref, xn_ref, rt_ref):
    ya = _dot(oa_ref[...], woa_ref[...])
    ob = ob_ref[...]
    obv = jnp.concatenate(
        [_dot(ob[:, h * KV_LORA:(h + 1) * KV_LORA], wuv_ref[h]) for h in range(H_B)], axis=1)
    yb = _dot(obv.astype(bf16), wob_ref[...])
    z = jax.nn.sigmoid(ga_ref[...]) * ya + jax.nn.sigmoid(gb_ref[...]) * yb
    h1 = h_ref[...] + _dot(z.astype(bf16), wo_ref[...])
    h1_ref[...] = h1
    xn = _rms(h1, gf_ref[...])
    for j, blk in enumerate(_split_rows(xn)):
        xn_ref[:, j, :] = blk
    rt_ref[...] = _route_tile(_dot(xn.astype(bf16), wr_ref[...]))


def _merge(oa, ob, proj, h, woa, wuv, wob, wo, g_ffn, wr):
    n, d = h.shape
    tm = ROW_TILE
    row = lambda w: pl.BlockSpec((tm, w), lambda i: (i, 0))
    jga, jgb = _COL["ga"][0], _COL["gb"][0]
    return pl.pallas_call(
        _merge_kernel,
        grid=(n // tm,),
        in_specs=[row(oa.shape[1]), row(ob.shape[1]),
                  pl.BlockSpec((tm, d), lambda i: (i, jga)), pl.BlockSpec((tm, d), lambda i: (i, jgb)),
                  row(d), _const_spec(woa.shape), _const_spec(wuv.shape), _const_spec(wob.shape),
                  _const_spec(wo.shape), _const_spec((1, d)), _const_spec(wr.shape)],
        out_specs=[row(d), pl.BlockSpec((tm, d // LANE, LANE), lambda i: (i, 0, 0)), row(LANE)],
        out_shape=[jax.ShapeDtypeStruct((n, d), f32), jax.ShapeDtypeStruct((n, d // LANE, LANE), f32),
                   jax.ShapeDtypeStruct((n, LANE), f32)],
        compiler_params=_cparams("parallel"),
        name="merge_router",
    )(oa, ob, proj, proj, h, woa, wuv, wob, wo, g_ffn.reshape(1, d), wr)


def _gather_rows_start(idx_ref, base, n, src_hbm, dst, sem):
    def issue(r, c):
        pltpu.make_async_copy(src_hbm.at[idx_ref[base + r]], dst.at[:, r, :], sem).start()
        return c

    lax.fori_loop(0, n, issue, 0, unroll=8)


def _gather_rows_wait(n, src_hbm, dst, sem):
    def drain(r, c):
        pltpu.make_async_copy(src_hbm.at[0], dst.at[:, r, :], sem).wait()
        return c

    lax.fori_loop(0, n, drain, 0, unroll=8)


def _load_rows(buf):
    return jnp.concatenate([buf[j] for j in range(buf.shape[0])], axis=1)


def _moe_kernel(te_ref, src_ref, nu_ref, x_hbm, wt_ref, wg_ref, wu_ref, wd_ref, y_ref, xbuf, sems, *, tm):
    t = pl.program_id(0)
    nu = nu_ref[0]

    @pl.when(t == 0)
    def _():
        _gather_rows_start(src_ref, 0, tm, x_hbm, xbuf.at[0], sems.at[0])

    @pl.when(t + 1 < nu)
    def _():
        nxt = (t + 1) % 2
        _gather_rows_start(src_ref, (t + 1) * tm, tm, x_hbm, xbuf.at[nxt], sems.at[nxt])

    @pl.when(t < nu)
    def _():
        slot = t % 2
        _gather_rows_wait(tm, x_hbm, xbuf.at[slot], sems.at[slot])
        x = _load_rows(xbuf.at[slot]).astype(bf16)
        g = _dot(x, wg_ref[...].astype(bf16))
        u = _dot(x, wu_ref[...].astype(bf16))
        hh = (g * jax.nn.sigmoid(g)) * u * wt_ref[...]
        y = _dot(hh.astype(bf16), wd_ref[...].astype(bf16))
        for j, blk in enumerate(_split_rows(y)):
            y_ref[:, j, :] = blk

    @pl.when(t >= nu)
    def _():
        y_ref[...] = jnp.zeros_like(y_ref)


def _moe_experts(tile_expert, src_token, n_used, xn3, wt_sorted, w_gate, w_up, w_down, tm):
    p_total = src_token.shape[0]
    nl = xn3.shape[1]
    d = nl * LANE
    n_tiles = p_total // tm
    grid_spec = pltpu.PrefetchScalarGridSpec(
        num_scalar_prefetch=3,
        grid=(n_tiles,),
        in_specs=[pl.BlockSpec(memory_space=pl.ANY),
                  pl.BlockSpec((tm, 1), lambda t, te, src, nu: (t, 0)),
                  pl.BlockSpec((None, d, D_EXPERT), lambda t, te, src, nu: (te[t], 0, 0)),
                  pl.BlockSpec((None, d, D_EXPERT), lambda t, te, src, nu: (te[t], 0, 0)),
                  pl.BlockSpec((None, D_EXPERT, d), lambda t, te, src, nu: (te[t], 0, 0))],
        out_specs=pl.BlockSpec((tm, nl, LANE), lambda t, te, src, nu: (t, 0, 0)),
        scratch_shapes=[pltpu.VMEM((2, nl, tm, LANE), f32), pltpu.SemaphoreType.DMA((2,))])
    return pl.pallas_call(
        functools.partial(_moe_kernel, tm=tm),
        grid_spec=grid_spec,
        out_shape=jax.ShapeDtypeStruct((p_total, nl, LANE), f32),
        compiler_params=_cparams("arbitrary"),
        name="moe_experts",
    )(tile_expert, src_token, n_used, xn3, wt_sorted, w_gate, w_up, w_down)


def _combine_kernel(p0_ref, p1_ref, h_ref, g_ref, ys_hbm, o_ref, buf, sems, *, tm):
    i = pl.program_id(0)
    n_steps = pl.num_programs(0)

    def fetch(step, slot):
        _gather_rows_start(p0_ref, step * tm, tm, ys_hbm, buf.at[slot, 0], sems.at[slot, 0])
        _gather_rows_start(p1_ref, step * tm, tm, ys_hbm, buf.at[slot, 1], sems.at[slot, 1])

    @pl.when(i == 0)
    def _():
        fetch(0, 0)

    @pl.when(i + 1 < n_steps)
    def _():
        fetch(i + 1, (i + 1) % 2)

    slot = i % 2
    _gather_rows_wait(tm, ys_hbm, buf.at[slot, 0], sems.at[slot, 0])
    _gather_rows_wait(tm, ys_hbm, buf.at[slot, 1], sems.at[slot, 1])
    h2 = h_ref[...] + (_load_rows(buf.at[slot, 0]) + _load_rows(buf.at[slot, 1]))
    o_ref[...] = _rms(h2, g_ref[...])


def _combine(pos0, pos1, h1, g_final, y_sorted):
    n, d = h1.shape
    nl = d // LANE
    tm = ROW_TILE
    grid_spec = pltpu.PrefetchScalarGridSpec(
        num_scalar_prefetch=2,
        grid=(n // tm,),
        in_specs=[pl.BlockSpec((tm, d), lambda i, a, b: (i, 0)),
                  pl.BlockSpec((1, d), lambda i, a, b: (0, 0)),
                  pl.BlockSpec(memory_space=pl.ANY)],
        out_specs=pl.BlockSpec((tm, d), lambda i, a, b: (i, 0)),
        scratch_shapes=[pltpu.VMEM((2, 2, nl, tm, LANE), f32), pltpu.SemaphoreType.DMA((2, 2))])
    return pl.pallas_call(
        functools.partial(_combine_kernel, tm=tm),
        grid_spec=grid_spec,
        out_shape=jax.ShapeDtypeStruct((n, d), f32),
        compiler_params=_cparams("arbitrary"),
        name="moe_combine_norm",
    )(pos0, pos1, h1, g_final.reshape(1, d), y_sorted)


_IN_SIZES = (H_A * DH_A, KVH_A * DH_A, KVH_A * DH_A, H_IDX * D_IDX, H_IDX, D_IDX,
             Q_LORA, KV_LORA, D_ROPE, None, None)


def _pack_w_in(w_in, d):
    sizes = [d if s is None else s for s in _IN_SIZES]
    offs = np.cumsum([0] + sizes)
    qa, ka, va, qi, wi, ki, dq, dkv, krr, ga, gb = [w_in[:, offs[j]:offs[j + 1]] for j in range(11)]
    padw = lambda a, w: jnp.pad(a, ((0, 0), (0, w - a.shape[1])))
    half = D_ROPE // 2
    krs = jnp.concatenate([krr[:, half:], krr[:, :half]], axis=1)
    parts = dict(ga=ga, gb=gb, qa=qa, qi=qi, dq=dq, ka=ka, va=va, dkv=dkv, wi=padw(wi, LANE),
                 ki=padw(ki, LANE), krr=padw(krr, LANE), krs=padw(krs, LANE))
    cols = [parts[n] for n, _ in _SEGS]
    used = sum(w for _, w in _SEGS)
    cols.append(jnp.zeros((w_in.shape[0], PROJ_W - used), w_in.dtype))
    return jnp.concatenate(cols, axis=1).astype(bf16)


def _pack_w_uq(w_uq):
    w = w_uq.reshape(Q_LORA, H_B, D_NOPE + D_ROPE)
    nope = w[:, :, :D_NOPE].reshape(Q_LORA, H_B * D_NOPE)
    rope = w[:, :, D_NOPE:]
    half = D_ROPE // 2
    rope_sw = jnp.concatenate([rope[:, :, half:], rope[:, :, :half]], axis=2)
    padr = lambda a: jnp.pad(a, ((0, 0), (0, 0), (0, LANE - D_ROPE))).reshape(Q_LORA, H_B * LANE)
    return jnp.concatenate([nope, padr(rope), padr(rope_sw)], axis=1).astype(bf16)


def _rope_tables(pos):
    inv = ROPE_THETA ** (-jnp.arange(0, D_ROPE, 2, dtype=f32) / D_ROPE)
    ang = pos.astype(f32)[:, None] * inv[None, :]
    cos, sin = jnp.cos(ang), jnp.sin(ang)
    z = jnp.zeros((pos.shape[0], LANE - D_ROPE), f32)
    return jnp.concatenate([cos, cos, z], axis=1), jnp.concatenate([-sin, sin, z], axis=1)


def _rel_bucket(dist):
    n = jnp.maximum(dist, 0)
    max_exact = N_BUCKETS // 2
    nf = jnp.maximum(n, 1).astype(f32)
    large = max_exact + (jnp.log(nf / max_exact) / math.log(MAX_DIST / max_exact)
                         * (N_BUCKETS - max_exact)).astype(i32)
    large = jnp.minimum(large, N_BUCKETS - 1)
    return jnp.where(n < max_exact, n, large)


def _bias_of_dist(bias_tab, dist):
    return jnp.moveaxis(bias_tab[jnp.clip(dist, 0, MAX_DIST)], -1, 0)


def _sort_slots(eid, wts, tm):
    n = eid.shape[0]
    a = n * TOP_E
    e_flat = eid.reshape(a)
    onehot = (e_flat[:, None] == jnp.arange(N_EXPERTS, dtype=i32)[None, :]).astype(i32)
    rank = jnp.sum((jnp.cumsum(onehot, axis=0) - onehot) * onehot, axis=1)
    counts = jnp.sum(onehot, axis=0)
    padded = (counts + tm - 1) // tm * tm
    ends = jnp.cumsum(padded)
    starts = ends - padded
    pos = starts[e_flat] + rank
    p_total = (a + tm - 1) // tm * tm + N_EXPERTS * tm
    src_token = jnp.zeros((p_total,), i32).at[pos].set(jnp.arange(a, dtype=i32) // TOP_E)
    wt_sorted = jnp.zeros((p_total,), f32).at[pos].set(wts.reshape(a))
    tile_start = jnp.arange(p_total // tm, dtype=i32) * tm
    tile_expert = jnp.minimum(jnp.searchsorted(ends, tile_start, side="right"), N_EXPERTS - 1).astype(i32)
    n_used = (ends[-1] // tm).astype(i32).reshape(1)
    pos2 = pos.reshape(n, TOP_E)
    return tile_expert, src_token, n_used, wt_sorted.reshape(p_total, 1), pos2[:, 0], pos2[:, 1]


def _pages_per_chunk(n_pages):
    pp = 16
    while n_pages % pp:
        pp //= 2
    return pp


def kernel(x_prompt, x_sample, cache_k, cache_v, cache_idx_k, cache_ckv, cache_kr, page_table,
           meta_tokens, rel_bias, g_attn, w_in, g_q, w_uq, g_kv, w_uk, w_uv, w_oa, w_ob, w_o,
           g_ffn, w_rg, w_re, w_gate, w_up, w_down, g_final):
    nb, s_len, d = x_prompt.shape
    bs, ts, _ = x_sample.shape
    depth = w_in.shape[0]
    assert depth == 1 and ts <= SUBLANE
    rs = 4 if ts <= 4 else SUBLANE
    pack = SUBLANE // rs
    assert bs % pack == 0
    t_len = s_len + N_META
    tp = -(-t_len // QB) * QB
    n_pages = page_table.shape[1]
    past = n_pages * PAGE
    topk_p = min(TOPK_MAX, s_len // 4)
    topk_s = min(TOPK_MAX, (past + ts) // 4)
    n_prompt = nb * tp
    n_tok = n_prompt + bs * ts
    n_pad = -(-n_tok // ROW_TILE) * ROW_TILE
    tm_proj = 512 if n_pad % 512 == 0 else ROW_TILE

    hp = jnp.concatenate([jnp.broadcast_to(meta_tokens.astype(x_prompt.dtype)[None], (nb, N_META, d)),
                          x_prompt, jnp.zeros((nb, tp - t_len, d), x_prompt.dtype)], axis=1)
    h0 = jnp.concatenate([hp.reshape(n_prompt, d), x_sample.reshape(bs * ts, d),
                          jnp.zeros((n_pad - n_tok, d), x_prompt.dtype)], axis=0)
    pos = jnp.concatenate([jnp.tile(jnp.arange(tp, dtype=i32), nb),
                           jnp.tile(past + jnp.arange(ts, dtype=i32), bs),
                           jnp.zeros((n_pad - n_tok,), i32)])
    cos_t, sin_t = _rope_tables(pos)

    l = 0
    w_pack = _pack_w_in(w_in[l], d)
    wuq_p = _pack_w_uq(w_uq[l])
    wuk_t = jnp.transpose(w_uk[l], (1, 2, 0)).astype(bf16)
    wuv_p = jnp.transpose(w_uv[l], (1, 0, 2)).astype(bf16)
    wr = jnp.pad(jnp.concatenate([w_rg[l], w_re[l]], axis=1),
                 ((0, 0), (0, LANE - N_GROUPS - N_EXPERTS))).astype(bf16)

    proj = _norm_matmul(h0, g_attn[l], w_pack, tm_proj, 1024)
    qcat, ckv, kr, kcat, kab, vab, kib = _post_project(proj, cos_t, sin_t, g_q[l], g_kv[l], wuq_p, wuk_t)

    bias_tab = rel_bias[_rel_bucket(jnp.arange(MAX_DIST + 1, dtype=i32))].astype(f32)
    tq = jnp.arange(QB, dtype=i32)[:, None]
    band = _bias_of_dist(bias_tab, tq + QB - jnp.arange(BAND, dtype=i32)[None, :])
    far1 = bias_tab[MAX_DIST]
    far = jnp.broadcast_to(far1[:, None, None], (H_A, 1, LANE))
    oa_p = _dsa_prompt(proj, kib, kab, vab, band, far, nb, tp, topk_p)
    ob_p = _mla_prompt(qcat, kcat, nb, tp)

    pp = _pages_per_chunk(n_pages)
    ch = pp * PAGE
    nbp = bs // pack
    srow = lambda a: a[n_prompt:n_prompt + bs * ts]
    seg = lambda name: srow(proj)[:, _COL[name][0] * _COL[name][1]:(_COL[name][0] + 1) * _COL[name][1]]

    def pack_rows(a, lead):
        feat = a.shape[1:]
        a = a.reshape((nbp, pack, ts) + feat)
        a = jnp.pad(a, ((0, 0), (0, 0), (0, rs - ts)) + ((0, 0),) * len(feat))
        nl = len(lead)
        a = jnp.transpose(a, (0,) + tuple(range(3, 3 + nl)) + (1, 2, 3 + nl))
        return a.reshape((nbp,) + lead + (SUBLANE, feat[-1]))

    qi_s = pack_rows(seg("qi").reshape(bs * ts, H_IDX, D_IDX), (H_IDX,)).reshape(nbp, H_IDX * SUBLANE, D_IDX)
    wi_s = pack_rows(seg("wi")[:, :H_IDX].reshape(bs * ts, H_IDX, 1), (H_IDX,)).reshape(nbp, H_IDX * SUBLANE, 1)
    qa_s = pack_rows(seg("qa").reshape(bs * ts, KVH_A, REP_A, DH_A), (KVH_A, REP_A))
    qa_s = qa_s.reshape(nbp, KVH_A, REP_A * SUBLANE, DH_A)
    nblk_s = -(-(bs * ts) // QB)
    qc_s = qcat[n_prompt // QB:n_prompt // QB + nblk_s].transpose(0, 2, 1, 3)
    qc_s = qc_s.reshape(nblk_s * QB, H_B, MLA_KCAT)[:bs * ts].reshape(bs, ts, H_B, MLA_KCAT)
    qc_s = jnp.pad(qc_s, ((0, 0), (0, SUBLANE - ts), (0, 0), (0, 0))).transpose(0, 2, 1, 3)
    qc_s = qc_s.reshape(bs, H_B * SUBLANE, MLA_KCAT)
    padp = lambda a: jnp.pad(a.reshape(bs, ts, a.shape[-1]), ((0, 0), (0, PAGE - ts), (0, 0)))
    ki_new = padp(srow(kib)[:, :D_IDX])
    kc_new = padp(srow(kcat))
    ka_new = padp(srow(kab))
    va_new = padp(srow(vab))
    idx_t = jnp.swapaxes(cache_idx_k, 2, 3)
    kr_t = jnp.swapaxes(cache_kr, 2, 3)
    scores, ob_s = _sample_pass_a(page_table, qi_s, wi_s * IDX_W_SCALE, qc_s, ki_new, kc_new,
                                  idx_t, cache_ckv, kr_t, ts, rs, pp)
    thr = _sample_thresholds(scores, topk_s)
    j8 = (jnp.arange(SUBLANE, dtype=i32) % rs)[:, None]
    near = min(ch, BAND)
    bias_last = _bias_of_dist(bias_tab, j8 + near - jnp.arange(near, dtype=i32)[None, :])
    if ch > near:
        bias_last = jnp.concatenate(
            [jnp.broadcast_to(far1[:, None, None], (H_A, SUBLANE, ch - near)), bias_last], axis=2)
    bias_new = _bias_of_dist(bias_tab, j8 - jnp.arange(PAGE, dtype=i32)[None, :])
    shp = lambda a: a.reshape((KVH_A, REP_A) + a.shape[1:])
    oa_s = _sample_pass_b(page_table, scores, thr, qa_s, ka_new, va_new, shp(bias_last), shp(bias_new),
                          shp(far1[:, None, None]), cache_k, cache_v, rs, pp)
    oa_s = oa_s.reshape(nbp, pack, KVH_A, REP_A, pack, rs, DH_A)
    oa_s = jnp.moveaxis(jnp.diagonal(oa_s, axis1=1, axis2=4), -1, 1)
    oa_s = oa_s[:, :, :, :, :ts].transpose(0, 1, 4, 2, 3, 5).reshape(bs * ts, H_A * DH_A).astype(bf16)
    ob_s = ob_s.reshape(bs, H_B, SUBLANE, KV_LORA)[:, :, :ts].transpose(0, 2, 1, 3)
    ob_s = ob_s.reshape(bs * ts, H_B * KV_LORA).astype(bf16)
    tail = lambda w: jnp.zeros((n_pad - n_tok, w), bf16)
    oa = jnp.concatenate([oa_p, oa_s, tail(H_A * DH_A)], axis=0)
    ob = jnp.concatenate([ob_p, ob_s, tail(H_B * KV_LORA)], axis=0)

    h1, xn3, route = _merge(oa, ob, proj, h0, w_oa[l].astype(bf16), wuv_p, w_ob[l].astype(bf16),
                            w_o[l].astype(bf16), g_ffn[l], wr)
    eid = route[:, 0:TOP_E].astype(i32)
    wts = route[:, TOP_E:2 * TOP_E]
    tile_expert, src_token, n_used, wt_sorted, pos0, pos1 = _sort_slots(eid, wts, ROW_TILE)
    y_sorted = _moe_experts(tile_expert, src_token, n_used, xn3, wt_sorted, w_gate[l], w_up[l], w_down[l],
                            ROW_TILE)
    y = _combine(pos0, pos1, h1, g_final, y_sorted)

    y_prompt = y[:n_prompt].reshape(nb, tp, d)[:, N_META:t_len]
    y_sample = y[n_prompt:n_tok].reshape(bs, ts, d)
    seg_all = lambda name: proj[:, _COL[name][0] * _COL[name][1]:(_COL[name][0] + 1) * _COL[name][1]]

    def states(rows, lead):
        ka = rows(seg_all("ka")).reshape(lead + (KVH_A, DH_A))
        va = rows(seg_all("va")).reshape(lead + (KVH_A, DH_A))
        ki = rows(seg_all("ki"))[..., :D_IDX].reshape(lead + (D_IDX,))
        cc = rows(ckv).reshape(lead + (KV_LORA,))
        rr = rows(kr)[..., :D_ROPE].reshape(lead + (D_ROPE,))
        return [a[None] for a in (ka, va, ki, cc, rr)]

    prow = lambda a: a[:n_prompt].reshape(nb, tp, a.shape[-1])[:, :t_len]
    st_p = states(prow, (nb, t_len))
    st_s = states(lambda a: a[n_prompt:n_tok], (bs, ts))
    return (y_prompt, y_sample, *st_p, *st_s)
```

```python
import functools
import math

import numpy as np
import jax
import jax.numpy as jnp
from jax import lax
from jax.experimental import pallas as pl
from jax.experimental.pallas import tpu as pltpu

N_META = 16
H_A, KVH_A, DH_A = 8, 2, 128
REP_A = H_A // KVH_A
H_IDX, D_IDX = 16, 64
TOPK_MAX = 256
IDX_W_SCALE = (H_IDX ** -0.5) * (D_IDX ** -0.5)
N_BUCKETS, MAX_DIST = 32, 128
H_B, Q_LORA, KV_LORA, D_NOPE, D_ROPE, D_V = 8, 512, 256, 128, 64, 128
ROPE_THETA = 10000.0
MLA_SCALE = (D_NOPE + D_ROPE) ** -0.5
DSA_SCALE = DH_A ** -0.5
N_GROUPS, E_PER_GROUP, TOP_E, D_EXPERT = 4, 8, 2, 512
N_EXPERTS = N_GROUPS * E_PER_GROUP
EPS = 1e-6
PAGE = 128

LANE = 128
SUBLANE = 8
VMEM_LIMIT_BYTES = 56 * 1024 * 1024

QB = 128
ROW_TILE = 256
MLA_KCAT = KV_LORA + LANE
BAND = 2 * QB
INT_MIN = -2 ** 31
NEG_INF_KEY = -2 ** 31 + 0x7FFFFF

_SEGS = (("ga", 2048), ("gb", 2048), ("qa", 1024), ("qi", 1024), ("dq", 512), ("ka", 256),
         ("va", 256), ("dkv", 256), ("wi", 128), ("ki", 128), ("krr", 128), ("krs", 128))
_COL = {}
_o = 0
for _n, _w in _SEGS:
    assert _o % _w == 0
    _COL[_n] = (_o // _w, _w)
    _o += _w
PROJ_W = 8192
assert _o <= PROJ_W

f32, bf16, i32 = jnp.float32, jnp.bfloat16, jnp.int32


def _cparams(*sem):
    return pltpu.CompilerParams(dimension_semantics=sem, vmem_limit_bytes=VMEM_LIMIT_BYTES)


def _dot(a, b):
    return jnp.dot(a, b, preferred_element_type=f32)


def _dot_nt(a, b):
    return lax.dot_general(a, b, (((1,), (1,)), ((), ())), preferred_element_type=f32)


def _rms(x, g):
    return x * lax.rsqrt(jnp.mean(x * x, axis=-1, keepdims=True) + EPS) * g


def _const_spec(shape):
    nd = len(shape)
    return pl.BlockSpec(shape, lambda *_: (0,) * nd, pipeline_mode=pl.Buffered(1))


def _split_rows(x):
    return [x[:, j * LANE:(j + 1) * LANE] for j in range(x.shape[1] // LANE)]


def _norm_matmul_kernel(x_ref, g_ref, w_ref, o_ref, xn_ref):
    @pl.when(pl.program_id(1) == 0)
    def _():
        xn_ref[...] = _rms(x_ref[...], g_ref[...]).astype(bf16)

    o_ref[...] = _dot(xn_ref[...], w_ref[...])


def _norm_matmul(x, g, w, tm, tn):
    n, k = x.shape
    m = w.shape[1]
    return pl.pallas_call(
        _norm_matmul_kernel,
        grid=(n // tm, m // tn),
        in_specs=[pl.BlockSpec((tm, k), lambda i, j: (i, 0)),
                  pl.BlockSpec((1, k), lambda i, j: (0, 0)),
                  pl.BlockSpec((k, tn), lambda i, j: (0, j))],
        out_specs=pl.BlockSpec((tm, tn), lambda i, j: (i, j)),
        out_shape=jax.ShapeDtypeStruct((n, m), f32),
        scratch_shapes=[pltpu.VMEM((tm, k), bf16)],
        compiler_params=_cparams("parallel", "arbitrary"),
        name="in_proj",
    )(x, g.reshape(1, k), w)


def _post_kernel(dq_ref, dkv_ref, krr_ref, krs_ref, ka_ref, va_ref, ki_ref, cos_ref, sin_ref,
                 gq_ref, gkv_ref, wuq_ref, wuk_ref,
                 qcat_ref, ckv_ref, kr_ref, kcat_ref, kab_ref, vab_ref, kib_ref):
    tm = dq_ref.shape[0]
    cos_t = cos_ref[...]
    sin_t = sin_ref[...]
    dqn = _rms(dq_ref[...], gq_ref[...]).astype(bf16)
    q = _dot(dqn, wuq_ref[...])
    for h in range(H_B):
        qlat = _dot(q[:, h * 128:(h + 1) * 128].astype(bf16), wuk_ref[h])
        qr = (q[:, 1024 + h * 128:1024 + (h + 1) * 128] * cos_t
              + q[:, 2048 + h * 128:2048 + (h + 1) * 128] * sin_t)
        for sb in range(tm // QB):
            qcat_ref[sb, h, :, 0:KV_LORA] = qlat[sb * QB:(sb + 1) * QB].astype(bf16)
            qcat_ref[sb, h, :, KV_LORA:MLA_KCAT] = qr[sb * QB:(sb + 1) * QB].astype(bf16)
    ckv = _rms(dkv_ref[...], gkv_ref[...])
    ckv_ref[...] = ckv
    kr = krr_ref[...] * cos_t + krs_ref[...] * sin_t
    kr_ref[...] = kr
    kcat_ref[:, 0:KV_LORA] = ckv.astype(bf16)
    kcat_ref[:, KV_LORA:MLA_KCAT] = kr.astype(bf16)
    kab_ref[...] = ka_ref[...].astype(bf16)
    vab_ref[...] = va_ref[...].astype(bf16)
    kib_ref[...] = ki_ref[...].astype(bf16)


def _post_project(proj, cos_t, sin_t, g_q, g_kv, wuq_p, wuk_t):
    n = proj.shape[0]
    tm = ROW_TILE

    def col(name):
        j, w = _COL[name]
        return pl.BlockSpec((tm, w), lambda i, j=j: (i, j))

    row = lambda w: pl.BlockSpec((tm, w), lambda i: (i, 0))
    outs = pl.pallas_call(
        _post_kernel,
        grid=(n // tm,),
        in_specs=[col("dq"), col("dkv"), col("krr"), col("krs"), col("ka"), col("va"), col("ki"),
                  row(LANE), row(LANE),
                  _const_spec((1, Q_LORA)), _const_spec((1, KV_LORA)),
                  _const_spec(wuq_p.shape), _const_spec(wuk_t.shape)],
        out_specs=[pl.BlockSpec((tm // QB, H_B, QB, MLA_KCAT), lambda i: (i, 0, 0, 0)),
                   row(KV_LORA), row(LANE), row(MLA_KCAT), row(256), row(256), row(LANE)],
        out_shape=[jax.ShapeDtypeStruct((n // QB, H_B, QB, MLA_KCAT), bf16),
                   jax.ShapeDtypeStruct((n, KV_LORA), f32),
                   jax.ShapeDtypeStruct((n, LANE), f32),
                   jax.ShapeDtypeStruct((n, MLA_KCAT), bf16),
                   jax.ShapeDtypeStruct((n, 256), bf16),
                   jax.ShapeDtypeStruct((n, 256), bf16),
                   jax.ShapeDtypeStruct((n, LANE), bf16)],
        compiler_params=_cparams("parallel"),
        name="post_proj",
    )(proj, proj, proj, proj, proj, proj, proj, cos_t, sin_t,
      g_q.reshape(1, -1), g_kv.reshape(1, -1), wuq_p, wuk_t)
    return outs


def _mla_prompt_kernel(q_ref, k_ref, o_ref, *, tk):
    i = pl.program_id(1)
    tp = k_ref.shape[0]
    t0 = i * QB
    rows = H_B * QB
    q = q_ref[0].reshape(rows, MLA_KCAT)
    qpos = t0 + (lax.broadcasted_iota(i32, (rows, 1), 0) & (QB - 1))
    nk = (t0 + QB + tk - 1) // tk

    def body(c, carry):
        m, l, acc = carry
        lo = c * tk
        st = pl.multiple_of(jnp.minimum(lo, tp - tk), QB)
        ks = k_ref[pl.ds(st, tk), :]
        s = _dot_nt(q, ks) * MLA_SCALE
        kpos = st + lax.broadcasted_iota(i32, (1, tk), 1)
        s = jnp.where((kpos <= qpos) & (kpos >= lo), s, -jnp.inf)
        m_new = jnp.maximum(m, jnp.max(s, axis=1, keepdims=True))
        p = jnp.exp(s - m_new)
        alpha = jnp.exp(m - m_new)
        l = alpha * l + jnp.sum(p, axis=1, keepdims=True)
        acc = alpha * acc + _dot(p.astype(bf16), ks[:, 0:KV_LORA])
        return m_new, l, acc

    m0 = jnp.full((rows, 1), -jnp.inf, f32)
    l0 = jnp.zeros((rows, 1), f32)
    a0 = jnp.zeros((rows, KV_LORA), f32)
    _, l, acc = lax.fori_loop(0, nk, body, (m0, l0, a0))
    o = acc / l
    for h in range(H_B):
        o_ref[:, h * KV_LORA:(h + 1) * KV_LORA] = o[h * QB:(h + 1) * QB].astype(bf16)


def _mla_prompt(qcat, kcat, nb, tp):
    nq = tp // QB
    tk = min(512, tp)
    return pl.pallas_call(
        functools.partial(_mla_prompt_kernel, tk=tk),
        grid=(nb, nq),
        in_specs=[pl.BlockSpec((1, H_B, QB, MLA_KCAT), lambda b, i: (b * nq + i, 0, 0, 0)),
                  pl.BlockSpec((tp, MLA_KCAT), lambda b, i: (b, 0))],
        out_specs=pl.BlockSpec((QB, H_B * KV_LORA), lambda b, i: (b * nq + i, 0)),
        out_shape=jax.ShapeDtypeStruct((nb * tp, H_B * KV_LORA), bf16),
        compiler_params=_cparams("parallel", "arbitrary"),
        name="mla_prompt",
    )(qcat, kcat)


def _sortable(score):
    b = lax.bitcast_convert_type(score, i32)
    return jnp.where(b < 0, b ^ jnp.int32(0x7FFFFFFF), b)


def _kth_largest(key_ref, k, count_axes, shape1):
    def body(it, res_u):
        cand_u = res_u | lax.shift_left(jnp.int32(1), 31 - it)
        cand_s = cand_u ^ jnp.int32(INT_MIN)
        ge = (key_ref[...] >= cand_s).astype(i32)
        cnt = jnp.sum(ge, axis=count_axes, keepdims=True)
        return jnp.where(cnt >= k, cand_u, res_u)

    res_u = lax.fori_loop(0, 32, body, jnp.zeros(shape1, i32), unroll=4)
    return res_u ^ jnp.int32(INT_MIN)


def _online_update(carry, s, v):
    m, l, acc = carry
    m_new = jnp.maximum(m, jnp.max(s, axis=1, keepdims=True))
    m_safe = jnp.where(m_new == -jnp.inf, 0.0, m_new)
    p = jnp.exp(s - m_safe)
    alpha = jnp.exp(m - m_safe)
    l = alpha * l + jnp.sum(p, axis=1, keepdims=True)
    acc = alpha * acc + _dot(p.astype(bf16), v)
    return m_new, l, acc


def _dsa_prompt_kernel(qa_ref, qi_ref, wi_ref, ki_ref, ka_ref, va_ref, band_ref, far_ref, o_ref,
                       key_ref, qs_ref, qg_ref, thr_ref, *, topk):
    i = pl.program_id(1)
    t0 = i * QB
    tp = ki_ref.shape[0]
    nck = tp // QB
    wide = 2 * QB
    qpos = t0 + lax.broadcasted_iota(i32, (QB, 1), 0)

    qi = qi_ref[...]
    for h in range(H_IDX):
        qs_ref[h * QB:(h + 1) * QB, :] = qi[:, h * D_IDX:(h + 1) * D_IDX].astype(bf16)
    qa = qa_ref[...]
    for h in range(H_A):
        qg_ref[h * QB:(h + 1) * QB, :] = qa[:, h * DH_A:(h + 1) * DH_A].astype(bf16)
    wi = wi_ref[...] * IDX_W_SCALE

    def score_chunk(c, carry):
        st = pl.multiple_of(jnp.minimum(c * wide, tp - wide), QB)
        s_all = _dot_nt(qs_ref[...], ki_ref[pl.ds(st, wide), 0:D_IDX])
        score = jnp.zeros((QB, wide), f32)
        for h in range(H_IDX):
            score = score + wi[:, h:h + 1] * jnp.maximum(s_all[h * QB:(h + 1) * QB], 0.0)
        kpos = st + lax.broadcasted_iota(i32, (1, wide), 1)
        key = jnp.where(kpos <= qpos, _sortable(score), jnp.int32(NEG_INF_KEY))
        cj = st // QB
        key_ref[cj] = key[:, 0:QB]
        key_ref[cj + 1] = key[:, QB:wide]
        return carry

    lax.fori_loop(0, (i + 2) // 2, score_chunk, 0)

    def blank_chunk(c, carry):
        key_ref[c] = jnp.full((QB, QB), NEG_INF_KEY, i32)
        return carry

    lax.fori_loop(i + 1, nck, blank_chunk, 0)

    lo = 0
    for hi in sorted({-(-nck * part // 3) for part in (1, 2, 3)}):
        @pl.when((i >= lo) & (i < hi))
        def _():
            thr_ref[...] = _kth_largest(key_ref.at[0:hi], topk, (0, 2), (1, QB, 1))

        lo = hi
    thr = thr_ref[0]

    grows = REP_A * QB

    def attend(state, st, width, sel, bias_of):
        new = []
        for g in range(KVH_A):
            rs = slice(g * grows, (g + 1) * grows)
            lg = _dot_nt(qg_ref[rs, :], ka_ref[pl.ds(st, width), g * DH_A:(g + 1) * DH_A]) * DSA_SCALE
            x = jnp.concatenate(
                [jnp.where(sel, lg[r * QB:(r + 1) * QB] + bias_of(REP_A * g + r), -jnp.inf)
                 for r in range(REP_A)], axis=0)
            new.append(_online_update(state[g], x, va_ref[pl.ds(st, width), g * DH_A:(g + 1) * DH_A]))
        return tuple(new)

    def far_chunk(c, state):
        st = pl.multiple_of(c * wide, wide)
        kk = key_ref[pl.ds(2 * c, 2)]
        key = jnp.concatenate([kk[0], kk[1]], axis=1)
        kpos = st + lax.broadcasted_iota(i32, (1, wide), 1)
        return attend(state, st, wide, (key >= thr) & (kpos < t0 - QB), lambda h: far_ref[h][:, 0:1])

    init = (jnp.full((grows, 1), -jnp.inf, f32), jnp.zeros((grows, 1), f32), jnp.zeros((grows, DH_A), f32))
    state = lax.fori_loop(0, i // 2, far_chunk, (init,) * KVH_A)
    st_a = pl.multiple_of(jnp.maximum(t0 - QB, 0), QB)
    behind = (st_a + lax.broadcasted_iota(i32, (1, QB), 1)) < t0
    state = attend(state, st_a, QB, (key_ref[jnp.maximum(i - 1, 0)] >= thr) & behind,
                   lambda h: band_ref[h][:, 0:QB])
    diag = (t0 + lax.broadcasted_iota(i32, (1, QB), 1)) <= qpos
    state = attend(state, pl.multiple_of(t0, QB), QB, (key_ref[i] >= thr) & diag,
                   lambda h: band_ref[h][:, QB:BAND])
    for g in range(KVH_A):
        _, l, acc = state[g]
        o = acc / l
        for r in range(REP_A):
            h = REP_A * g + r
            o_ref[:, h * DH_A:(h + 1) * DH_A] = o[r * QB:(r + 1) * QB].astype(bf16)


def _dsa_prompt(proj, kib, kab, vab, band, far, nb, tp, topk):
    nq = tp // QB
    jqa, jqi, jwi = _COL["qa"][0], _COL["qi"][0], _COL["wi"][0]
    return pl.pallas_call(
        functools.partial(_dsa_prompt_kernel, topk=topk),
        grid=(nb, nq),
        in_specs=[pl.BlockSpec((QB, 1024), lambda b, i: (b * nq + i, jqa)),
                  pl.BlockSpec((QB, 1024), lambda b, i: (b * nq + i, jqi)),
                  pl.BlockSpec((QB, LANE), lambda b, i: (b * nq + i, jwi)),
                  pl.BlockSpec((tp, LANE), lambda b, i: (b, 0)),
                  pl.BlockSpec((tp, 256), lambda b, i: (b, 0)),
                  pl.BlockSpec((tp, 256), lambda b, i: (b, 0)),
                  _const_spec((H_A, QB, BAND)),
                  _const_spec((H_A, 1, LANE))],
        out_specs=pl.BlockSpec((QB, H_A * DH_A), lambda b, i: (b * nq + i, 0)),
        out_shape=jax.ShapeDtypeStruct((nb * tp, H_A * DH_A), bf16),
        scratch_shapes=[pltpu.VMEM((tp // QB, QB, QB), i32),
                        pltpu.VMEM((H_IDX * QB, D_IDX), bf16),
                        pltpu.VMEM((H_A * QB, DH_A), bf16),
                        pltpu.VMEM((1, QB, 1), i32)],
        compiler_params=_cparams("parallel", "arbitrary"),
        name="dsa_prompt",
    )(proj, proj, proj, kib, kab, vab, band, far)


def _page_copies(pt_ref, base, c, slot, pp, srcs, bufs, sems):
    out = []
    for p in range(pp):
        page = pt_ref[base + c * pp + p]
        for a, (src, buf) in enumerate(zip(srcs, bufs)):
            out.append(pltpu.make_async_copy(src(page), buf(slot, p), sems.at[slot, a]))
    return out


def _page_stream(pt_ref, n_pages, pp, srcs, bufs, sems):
    s_id = pl.program_id(0)
    n_seq = pl.num_programs(0)
    n_chunks = n_pages // pp

    def copies(seq, c, slot):
        return _page_copies(pt_ref, seq * n_pages, c, slot, pp, srcs, bufs, sems)

    def prologue():
        @pl.when(s_id == 0)
        def _():
            for cp in copies(0, 0, 0):
                cp.start()

    def advance(c):
        slot = (s_id * n_chunks + c) % 2

        @pl.when(c + 1 < n_chunks)
        def _():
            for cp in copies(s_id, c + 1, 1 - slot):
                cp.start()

        @pl.when((c + 1 == n_chunks) & (s_id + 1 < n_seq))
        def _():
            for cp in copies(s_id + 1, 0, 1 - slot):
                cp.start()

        for cp in copies(s_id, c, slot):
            cp.wait()
        return slot

    return prologue, advance


def _merge_softmax(a, b):
    (ma, la, acca), (mb, lb, accb) = a, b
    m = jnp.maximum(ma, mb)
    m_safe = jnp.where(m == -jnp.inf, 0.0, m)
    ea = jnp.exp(ma - m_safe)
    eb = jnp.exp(mb - m_safe)
    return m, la * ea + lb * eb, acca * ea + accb * eb


def _sample_a_kernel(pt_ref, qi_ref, wi_ref, qc_ref, kin_ref, kcn_ref, idx_hbm, ckv_hbm, kr_hbm,
                     sc_ref, ob_ref, ibuf, cbuf, rbuf, sems, *, n_pages, pp, ts, rs):
    s_id = pl.program_id(0)
    pack = SUBLANE // rs
    par = s_id % pack
    n_chunks = n_pages // pp
    ch = pp * PAGE
    srcs = (lambda pg: idx_hbm.at[0, pg], lambda pg: ckv_hbm.at[0, pg], lambda pg: kr_hbm.at[0, pg])
    bufs = (lambda sl, p: ibuf.at[sl, :, pl.ds(p * PAGE, PAGE)],
            lambda sl, p: cbuf.at[sl, pl.ds(p * PAGE, PAGE)],
            lambda sl, p: rbuf.at[sl, :, pl.ds(p * PAGE, PAGE)])
    prologue, advance = _page_stream(pt_ref, n_pages, pp, srcs, bufs, sems)

    qi = qi_ref[0].astype(bf16)
    w = wi_ref[0]
    qlat = qc_ref[0][:, 0:KV_LORA]
    qrope = qc_ref[0][:, KV_LORA:KV_LORA + D_ROPE]
    rows = H_B * SUBLANE

    def head_sum(s):
        s = jnp.maximum(s, 0.0) * w
        return jnp.sum(s.reshape(H_IDX, SUBLANE, s.shape[1]), axis=0)

    def put_scores(c, sc):
        for q in range(pack):
            @pl.when(par == q)
            def _():
                sc_ref[0, c, q * rs:(q + 1) * rs, :] = sc[q * rs:(q + 1) * rs]

    prologue()
    nsplit = 2 if ch >= 2 * LANE else 1
    part = ch // nsplit

    def body(c, carry):
        slot = advance(c)
        put_scores(c, head_sum(_dot(qi, ibuf[slot].astype(bf16))))
        new = []
        for k in range(nsplit):
            ck = cbuf[slot, k * part:(k + 1) * part].astype(bf16)
            rk = rbuf[slot, :, k * part:(k + 1) * part].astype(bf16)
            s = (_dot_nt(qlat, ck) + _dot(qrope, rk)) * MLA_SCALE
            new.append(_online_update(carry[k], s, ck))
        return tuple(new)

    init = (jnp.full((rows, 1), -jnp.inf, f32), jnp.zeros((rows, 1), f32),
            jnp.zeros((rows, KV_LORA), f32))
    parts = lax.fori_loop(0, n_chunks, body, (init,) * nsplit)
    carry = parts[0] if nsplit == 1 else _merge_softmax(parts[0], parts[1])

    kj = lax.broadcasted_iota(i32, (1, PAGE), 1)
    ok8 = (kj <= (lax.broadcasted_iota(i32, (SUBLANE, 1), 0) & (rs - 1))) & (kj < ts)
    sc_new = jnp.where(ok8, head_sum(_dot_nt(qi, kin_ref[0])), -jnp.inf)
    if ch > PAGE:
        sc_new = jnp.concatenate([sc_new, jnp.full((SUBLANE, ch - PAGE), -jnp.inf, f32)], axis=1)
    put_scores(n_chunks, sc_new)
    kcn = kcn_ref[0]
    okr = (kj <= (lax.broadcasted_iota(i32, (rows, 1), 0) & (SUBLANE - 1))) & (kj < ts)
    s = jnp.where(okr, _dot_nt(qc_ref[0], kcn) * MLA_SCALE, -jnp.inf)
    _, l, acc = _online_update(carry, s, kcn[:, 0:KV_LORA])
    ob_ref[0] = acc / l


def _sample_pass_a(page_table, qi_s, wi_s, qc_s, ki_new, kc_new, idx_t, cache_ckv, kr_t, ts, rs, pp):
    bs, n_pages = page_table.shape
    pack = SUBLANE // rs
    n_chunks = n_pages // pp
    ch = pp * PAGE
    blk = lambda *shape: pl.BlockSpec((1,) + shape, lambda s, pt: (s,) + (0,) * len(shape))
    pblk = lambda *shape: pl.BlockSpec((1,) + shape, lambda s, pt: (s // pack,) + (0,) * len(shape))
    any_spec = pl.BlockSpec(memory_space=pl.ANY)
    grid_spec = pltpu.PrefetchScalarGridSpec(
        num_scalar_prefetch=1,
        grid=(bs,),
        in_specs=[pblk(H_IDX * SUBLANE, D_IDX), pblk(H_IDX * SUBLANE, 1), blk(H_B * SUBLANE, MLA_KCAT),
                  blk(PAGE, D_IDX), blk(PAGE, MLA_KCAT), any_spec, any_spec, any_spec],
        out_specs=[pblk(n_chunks + 1, SUBLANE, ch), blk(H_B * SUBLANE, KV_LORA)],
        scratch_shapes=[pltpu.VMEM((2, D_IDX, ch), f32), pltpu.VMEM((2, ch, KV_LORA), f32),
                        pltpu.VMEM((2, D_ROPE, ch), f32), pltpu.SemaphoreType.DMA((2, 3))])
    return pl.pallas_call(
        functools.partial(_sample_a_kernel, n_pages=n_pages, pp=pp, ts=ts, rs=rs),
        grid_spec=grid_spec,
        out_shape=[jax.ShapeDtypeStruct((bs // pack, n_chunks + 1, SUBLANE, ch), f32),
                   jax.ShapeDtypeStruct((bs, H_B * SUBLANE, KV_LORA), f32)],
        compiler_params=_cparams("arbitrary"),
        name="sample_idx_mla",
    )(page_table.reshape(-1), qi_s, wi_s, qc_s, ki_new, kc_new, idx_t, cache_ckv, kr_t)


def _thr_kernel(sc_ref, thr_ref, key_ref, *, topk):
    g = sc_ref.shape[0]
    key_ref[...] = _sortable(sc_ref[...])
    thr = _kth_largest(key_ref, topk, (1, 3), (g, 1, SUBLANE, 1))
    thr_ref[...] = jnp.broadcast_to(thr[:, 0], (g, SUBLANE, LANE))


def _sample_thresholds(scores, topk):
    nblk, nc1, _, ch = scores.shape
    g = 8
    while nblk % g:
        g //= 2
    return pl.pallas_call(
        functools.partial(_thr_kernel, topk=topk),
        grid=(nblk // g,),
        in_specs=[pl.BlockSpec((g, nc1, SUBLANE, ch), lambda i: (i, 0, 0, 0))],
        out_specs=pl.BlockSpec((g, SUBLANE, LANE), lambda i: (i, 0, 0)),
        out_shape=jax.ShapeDtypeStruct((nblk, SUBLANE, LANE), i32),
        scratch_shapes=[pltpu.VMEM((g, nc1, SUBLANE, ch), i32)],
        compiler_params=_cparams("parallel"),
        name="sample_topk_thr",
    )(scores)


def _sample_b_kernel(pt_ref, sc_ref, thr_ref, qa_ref, kn_ref, vn_ref, bl_ref, bn_ref, far_ref, k_hbm, v_hbm,
                     o_ref, kbuf, vbuf, sems, *, n_pages, pp):
    n_chunks = n_pages // pp
    ch = pp * PAGE
    prow = PAGE * KVH_A
    srcs = (lambda pg: k_hbm.at[pg], lambda pg: v_hbm.at[pg])
    bufs = (lambda sl, p: kbuf.at[sl, pl.ds(p * prow, prow)], lambda sl, p: vbuf.at[sl, pl.ds(p * prow, prow)])
    prologue, advance = _page_stream(pt_ref, n_pages, pp, srcs, bufs, sems)
    prologue()
    thr = thr_ref[0][:, 0:1]
    rows = REP_A * SUBLANE
    qa = [qa_ref[0, g].astype(bf16) for g in range(KVH_A)]
    nsplit = 2 if ch >= 2 * LANE else 1
    part = ch // nsplit

    def attend(carry, sc, kk, vv, bias_of):
        sel = _sortable(sc) >= thr
        n = sel.shape[1]
        new = []
        for g in range(KVH_A):
            lg = _dot_nt(qa[g], kk(g)) * DSA_SCALE
            lg = lg.reshape(REP_A, SUBLANE, n) + bias_of(g)
            lg = jnp.where(sel[None], lg, -jnp.inf).reshape(rows, n)
            new.append(_online_update(carry[g], lg, vv(g)))
        return tuple(new)

    def body(c, carry):
        slot = advance(c)
        is_last = jnp.broadcast_to(c, (REP_A, SUBLANE, part)) == n_chunks - 1
        new = []
        for k in range(nsplit):
            lo = k * part
            head_rows = lambda buf, g: buf[slot, pl.ds(KVH_A * lo + g, part, stride=KVH_A), :].astype(bf16)
            new.append(attend(
                carry[k], sc_ref[0, c][:, lo:lo + part],
                lambda g: head_rows(kbuf, g), lambda g: head_rows(vbuf, g),
                lambda g: jnp.where(is_last, bl_ref[g][:, :, lo:lo + part], far_ref[g])))
        return tuple(new)

    init = (jnp.full((rows, 1), -jnp.inf, f32), jnp.zeros((rows, 1), f32), jnp.zeros((rows, DH_A), f32))
    parts = lax.fori_loop(0, n_chunks, body, ((init,) * KVH_A,) * nsplit)
    sel_new = _sortable(sc_ref[0, n_chunks][:, 0:PAGE]) >= thr
    for g in range(KVH_A):
        carry = parts[0][g] if nsplit == 1 else _merge_softmax(parts[0][g], parts[1][g])
        lg = _dot_nt(qa[g], kn_ref[0][:, g * DH_A:(g + 1) * DH_A]) * DSA_SCALE
        lg = lg.reshape(REP_A, SUBLANE, PAGE) + bn_ref[g]
        lg = jnp.where(sel_new[None], lg, -jnp.inf).reshape(rows, PAGE)
        _, l, acc = _online_update(carry, lg, vn_ref[0][:, g * DH_A:(g + 1) * DH_A])
        o_ref[0, g] = acc / l


def _sample_pass_b(page_table, scores, thr, qa_s, ka_new, va_new, bias_last, bias_new, far_s, cache_k, cache_v,
                   rs, pp):
    bs, n_pages = page_table.shape
    pack = SUBLANE // rs
    n_chunks = n_pages // pp
    ch = pp * PAGE
    blk = lambda *shape: pl.BlockSpec((1,) + shape, lambda s, pt: (s,) + (0,) * len(shape))
    pblk = lambda *shape: pl.BlockSpec((1,) + shape, lambda s, pt: (s // pack,) + (0,) * len(shape))
    cst = lambda shape: pl.BlockSpec(shape, lambda s, pt: (0,) * len(shape))
    any_spec = pl.BlockSpec(memory_space=pl.ANY)
    grid_spec = pltpu.PrefetchScalarGridSpec(
        num_scalar_prefetch=1,
        grid=(bs,),
        in_specs=[pblk(n_chunks + 1, SUBLANE, ch), pblk(SUBLANE, LANE), pblk(KVH_A, REP_A * SUBLANE, DH_A),
                  blk(PAGE, 256), blk(PAGE, 256),
                  cst((KVH_A, REP_A, SUBLANE, ch)), cst((KVH_A, REP_A, SUBLANE, PAGE)), cst((KVH_A, REP_A, 1, 1)),
                  any_spec, any_spec],
        out_specs=[blk(KVH_A, REP_A * SUBLANE, DH_A)],
        scratch_shapes=[pltpu.VMEM((2, KVH_A * ch, DH_A), f32), pltpu.VMEM((2, KVH_A * ch, DH_A), f32),
                        pltpu.SemaphoreType.DMA((2, 2))])
    return pl.pallas_call(
        functools.partial(_sample_b_kernel, n_pages=n_pages, pp=pp),
        grid_spec=grid_spec,
        out_shape=[jax.ShapeDtypeStruct((bs, KVH_A, REP_A * SUBLANE, DH_A), f32)],
        compiler_params=_cparams("arbitrary"),
        name="sample_dsa",
    )(page_table.reshape(-1), scores, thr, qa_s, ka_new, va_new, bias_last, bias_new, far_s, cache_k, cache_v)[0]


def _route_tile(lg):
    lane = lax.broadcasted_iota(i32, lg.shape, 1)
    lane_f = lane.astype(f32)
    first = lambda hit: jnp.min(jnp.where(hit, lane_f, float(LANE)), axis=1, keepdims=True)
    gl = jnp.where(lane < N_GROUPS, lg, -jnp.inf)
    gm = jnp.max(gl, axis=1, keepdims=True)
    g_top = 1.0 / jnp.sum(jnp.exp(gl - gm), axis=1, keepdims=True)
    lo = N_GROUPS + E_PER_GROUP * first(gl == gm)
    el = jnp.where((lane_f >= lo) & (lane_f < lo + E_PER_GROUP), lg, -jnp.inf)
    em1 = jnp.max(el, axis=1, keepdims=True)
    i1 = first(el == em1)
    el2 = jnp.where(lane_f == i1, -jnp.inf, el)
    em2 = jnp.max(el2, axis=1, keepdims=True)
    i2 = first(el2 == em2)
    r = jnp.exp(em2 - em1)
    w1 = g_top / (1.0 + r)
    w2 = g_top * r / (1.0 + r)
    out = jnp.where(lane == 0, i1 - N_GROUPS, 0.0)
    out = jnp.where(lane == 1, i2 - N_GROUPS, out)
    out = jnp.where(lane == 2, w1, out)
    return jnp.where(lane == 3, w2, out)


def _merge_kernel(oa_ref, ob_ref, ga_ref, gb_ref, h_ref, woa_ref, wuv_ref, wob_ref, wo_ref, gf_ref, wr_ref,
                  h1_ref, xn_ref, rt_ref):
    ya = _dot(oa_ref[...], woa_ref[...])
    ob = ob_ref[...]
    obv = jnp.concatenate(
        [_dot(ob[:, h * KV_LORA:(h + 1) * KV_LORA], wuv_ref[h]) for h in range(H_B)], axis=1)
    yb = _dot(obv.astype(bf16), wob_ref[...])
    z = jax.nn.sigmoid(ga_ref[...]) * ya + jax.nn.sigmoid(gb_ref[...]) * yb
    h1 = h_ref[...] + _dot(z.astype(bf16), wo_ref[...])
    h1_ref[...] = h1
    xn = _rms(h1, gf_ref[...])
    for j, blk in enumerate(_split_rows(xn)):
        xn_ref[:, j, :] = blk
    rt_ref[...] = _route_tile(_dot(xn.astype(bf16), wr_ref[...]))


def _merge(oa, ob, proj, h, woa, wuv, wob, wo, g_ffn, wr):
    n, d = h.shape
    tm = ROW_TILE
    row = lambda w: pl.BlockSpec((tm, w), lambda i: (i, 0))
    jga, jgb = _COL["ga"][0], _COL["gb"][0]
    return pl.pallas_call(
        _merge_kernel,
        grid=(n // tm,),
        in_specs=[row(oa.shape[1]), row(ob.shape[1]),
                  pl.BlockSpec((tm, d), lambda i: (i, jga)), pl.BlockSpec((tm, d), lambda i: (i, jgb)),
                  row(d), _const_spec(woa.shape), _const_spec(wuv.shape), _const_spec(wob.shape),
                  _const_spec(wo.shape), _const_spec((1, d)), _const_spec(wr.shape)],
        out_specs=[row(d), pl.BlockSpec((tm, d // LANE, LANE), lambda i: (i, 0, 0)), row(LANE)],
        out_shape=[jax.ShapeDtypeStruct((n, d), f32), jax.ShapeDtypeStruct((n, d // LANE, LANE), f32),
                   jax.ShapeDtypeStruct((n, LANE), f32)],
        compiler_params=_cparams("parallel"),
        name="merge_router",
    )(oa, ob, proj, proj, h, woa, wuv, wob, wo, g_ffn.reshape(1, d), wr)


def _gather_rows_start(idx_ref, base, n, src_hbm, dst, sem):
    def issue(r, c):
        pltpu.make_async_copy(src_hbm.at[idx_ref[base + r]], dst.at[:, r, :], sem).start()
        return c

    lax.fori_loop(0, n, issue, 0, unroll=8)


def _gather_rows_wait(n, src_hbm, dst, sem):
    def drain(r, c):
        pltpu.make_async_copy(src_hbm.at[0], dst.at[:, r, :], sem).wait()
        return c

    lax.fori_loop(0, n, drain, 0, unroll=8)


def _load_rows(buf):
    return jnp.concatenate([buf[j] for j in range(buf.shape[0])], axis=1)


def _moe_kernel(te_ref, src_ref, nu_ref, x_hbm, wt_ref, wg_ref, wu_ref, wd_ref, y_ref, xbuf, sems, *, tm):
    t = pl.program_id(0)
    nu = nu_ref[0]

    @pl.when(t == 0)
    def _():
        _gather_rows_start(src_ref, 0, tm, x_hbm, xbuf.at[0], sems.at[0])

    @pl.when(t + 1 < nu)
    def _():
        nxt = (t + 1) % 2
        _gather_rows_start(src_ref, (t + 1) * tm, tm, x_hbm, xbuf.at[nxt], sems.at[nxt])

    @pl.when(t < nu)
    def _():
        slot = t % 2
        _gather_rows_wait(tm, x_hbm, xbuf.at[slot], sems.at[slot])
        x = _load_rows(xbuf.at[slot]).astype(bf16)
        g = _dot(x, wg_ref[...].astype(bf16))
        u = _dot(x, wu_ref[...].astype(bf16))
        hh = (g * jax.nn.sigmoid(g)) * u * wt_ref[...]
        y = _dot(hh.astype(bf16), wd_ref[...].astype(bf16))
        for j, blk in enumerate(_split_rows(y)):
            y_ref[:, j, :] = blk

    @pl.when(t >= nu)
    def _():
        y_ref[...] = jnp.zeros_like(y_ref)


def _moe_experts(tile_expert, src_token, n_used, xn3, wt_sorted, w_gate, w_up, w_down, tm):
    p_total = src_token.shape[0]
    nl = xn3.shape[1]
    d = nl * LANE
    n_tiles = p_total // tm
    grid_spec = pltpu.PrefetchScalarGridSpec(
        num_scalar_prefetch=3,
        grid=(n_tiles,),
        in_specs=[pl.BlockSpec(memory_space=pl.ANY),
                  pl.BlockSpec((tm, 1), lambda t, te, src, nu: (t, 0)),
                  pl.BlockSpec((None, d, D_EXPERT), lambda t, te, src, nu: (te[t], 0, 0)),
                  pl.BlockSpec((None, d, D_EXPERT), lambda t, te, src, nu: (te[t], 0, 0)),
                  pl.BlockSpec((None, D_EXPERT, d), lambda t, te, src, nu: (te[t], 0, 0))],
        out_specs=pl.BlockSpec((tm, nl, LANE), lambda t, te, src, nu: (t, 0, 0)),
        scratch_shapes=[pltpu.VMEM((2, nl, tm, LANE), f32), pltpu.SemaphoreType.DMA((2,))])
    return pl.pallas_call(
        functools.partial(_moe_kernel, tm=tm),
        grid_spec=grid_spec,
        out_shape=jax.ShapeDtypeStruct((p_total, nl, LANE), f32),
        compiler_params=_cparams("arbitrary"),
        name="moe_experts",
    )(tile_expert, src_token, n_used, xn3, wt_sorted, w_gate, w_up, w_down)


def _combine_kernel(p0_ref, p1_ref, h_ref, g_ref, ys_hbm, o_ref, buf, sems, *, tm):
    i = pl.program_id(0)
    n_steps = pl.num_programs(0)

    def fetch(step, slot):
        _gather_rows_start(p0_ref, step * tm, tm, ys_hbm, buf.at[slot, 0], sems.at[slot, 0])
        _gather_rows_start(p1_ref, step * tm, tm, ys_hbm, buf.at[slot, 1], sems.at[slot, 1])

    @pl.when(i == 0)
    def _():
        fetch(0, 0)

    @pl.when(i + 1 < n_steps)
    def _():
        fetch(i + 1, (i + 1) % 2)

    slot = i % 2
    _gather_rows_wait(tm, ys_hbm, buf.at[slot, 0], sems.at[slot, 0])
    _gather_rows_wait(tm, ys_hbm, buf.at[slot, 1], sems.at[slot, 1])
    h2 = h_ref[...] + (_load_rows(buf.at[slot, 0]) + _load_rows(buf.at[slot, 1]))
    o_ref[...] = _rms(h2, g_ref[...])


def _combine(pos0, pos1, h1, g_final, y_sorted):
    n, d = h1.shape
    nl = d // LANE
    tm = ROW_TILE
    grid_spec = pltpu.PrefetchScalarGridSpec(
        num_scalar_prefetch=2,
        grid=(n // tm,),
        in_specs=[pl.BlockSpec((tm, d), lambda i, a, b: (i, 0)),
                  pl.BlockSpec((1, d), lambda i, a, b: (0, 0)),
                  pl.BlockSpec(memory_space=pl.ANY)],
        out_specs=pl.BlockSpec((tm, d), lambda i, a, b: (i, 0)),
        scratch_shapes=[pltpu.VMEM((2, 2, nl, tm, LANE), f32), pltpu.SemaphoreType.DMA((2, 2))])
    return pl.pallas_call(
        functools.partial(_combine_kernel, tm=tm),
        grid_spec=grid_spec,
        out_shape=jax.ShapeDtypeStruct((n, d), f32),
        compiler_params=_cparams("arbitrary"),
        name="moe_combine_norm",
    )(pos0, pos1, h1, g_final.reshape(1, d), y_sorted)


_IN_SIZES = (H_A * DH_A, KVH_A * DH_A, KVH_A * DH_A, H_IDX * D_IDX, H_IDX, D_IDX,
             Q_LORA, KV_LORA, D_ROPE, None, None)


def _pack_w_in(w_in, d):
    sizes = [d if s is None else s for s in _IN_SIZES]
    offs = np.cumsum([0] + sizes)
    qa, ka, va, qi, wi, ki, dq, dkv, krr, ga, gb = [w_in[:, offs[j]:offs[j + 1]] for j in range(11)]
    padw = lambda a, w: jnp.pad(a, ((0, 0), (0, w - a.shape[1])))
    half = D_ROPE // 2
    krs = jnp.concatenate([krr[:, half:], krr[:, :half]], axis=1)
    parts = dict(ga=ga, gb=gb, qa=qa, qi=qi, dq=dq, ka=ka, va=va, dkv=dkv, wi=padw(wi, LANE),
                 ki=padw(ki, LANE), krr=padw(krr, LANE), krs=padw(krs, LANE))
    cols = [parts[n] for n, _ in _SEGS]
    used = sum(w for _, w in _SEGS)
    cols.append(jnp.zeros((w_in.shape[0], PROJ_W - used), w_in.dtype))
    return jnp.concatenate(cols, axis=1).astype(bf16)


def _pack_w_uq(w_uq):
    w = w_uq.reshape(Q_LORA, H_B, D_NOPE + D_ROPE)
    nope = w[:, :, :D_NOPE].reshape(Q_LORA, H_B * D_NOPE)
    rope = w[:, :, D_NOPE:]
    half = D_ROPE // 2
    rope_sw = jnp.concatenate([rope[:, :, half:], rope[:, :, :half]], axis=2)
    padr = lambda a: jnp.pad(a, ((0, 0), (0, 0), (0, LANE - D_ROPE))).reshape(Q_LORA, H_B * LANE)
    return jnp.concatenate([nope, padr(rope), padr(rope_sw)], axis=1).astype(bf16)


def _rope_tables(pos):
    inv = ROPE_THETA ** (-jnp.arange(0, D_ROPE, 2, dtype=f32) / D_ROPE)
    ang = pos.astype(f32)[:, None] * inv[None, :]
    cos, sin = jnp.cos(ang), jnp.sin(ang)
    z = jnp.zeros((pos.shape[0], LANE - D_ROPE), f32)
    return jnp.concatenate([cos, cos, z], axis=1), jnp.concatenate([-sin, sin, z], axis=1)


def _rel_bucket(dist):
    n = jnp.maximum(dist, 0)
    max_exact = N_BUCKETS // 2
    nf = jnp.maximum(n, 1).astype(f32)
    large = max_exact + (jnp.log(nf / max_exact) / math.log(MAX_DIST / max_exact)
                         * (N_BUCKETS - max_exact)).astype(i32)
    large = jnp.minimum(large, N_BUCKETS - 1)
    return jnp.where(n < max_exact, n, large)


def _bias_of_dist(bias_tab, dist):
    return jnp.moveaxis(bias_tab[jnp.clip(dist, 0, MAX_DIST)], -1, 0)


def _sort_slots(eid, wts, tm):
    n = eid.shape[0]
    a = n * TOP_E
    e_flat = eid.reshape(a)
    onehot = (e_flat[:, None] == jnp.arange(N_EXPERTS, dtype=i32)[None, :]).astype(i32)
    rank = jnp.sum((jnp.cumsum(onehot, axis=0) - onehot) * onehot, axis=1)
    counts = jnp.sum(onehot, axis=0)
    padded = (counts + tm - 1) // tm * tm
    ends = jnp.cumsum(padded)
    starts = ends - padded
    pos = starts[e_flat] + rank
    p_total = (a + tm - 1) // tm * tm + N_EXPERTS * tm
    tok = (jnp.arange(a, dtype=i32) // TOP_E).astype(f32)
    slots = jnp.zeros((p_total, 2), f32).at[pos].set(jnp.stack([tok, wts.reshape(a)], axis=1))
    src_token = slots[:, 0].astype(i32)
    wt_sorted = slots[:, 1]
    tile_start = jnp.arange(p_total // tm, dtype=i32) * tm
    tile_expert = jnp.minimum(jnp.searchsorted(ends, tile_start, side="right"), N_EXPERTS - 1).astype(i32)
    n_used = (ends[-1] // tm).astype(i32).reshape(1)
    pos2 = pos.reshape(n, TOP_E)
    return tile_expert, src_token, n_used, wt_sorted.reshape(p_total, 1), pos2[:, 0], pos2[:, 1]


def _pages_per_chunk(n_pages):
    pp = 16
    while n_pages % pp:
        pp //= 2
    return pp


def kernel(x_prompt, x_sample, cache_k, cache_v, cache_idx_k, cache_ckv, cache_kr, page_table,
           meta_tokens, rel_bias, g_attn, w_in, g_q, w_uq, g_kv, w_uk, w_uv, w_oa, w_ob, w_o,
           g_ffn, w_rg, w_re, w_gate, w_up, w_down, g_final):
    nb, s_len, d = x_prompt.shape
    bs, ts, _ = x_sample.shape
    depth = w_in.shape[0]
    assert depth == 1 and ts <= SUBLANE
    rs = 4 if ts <= 4 else SUBLANE
    pack = SUBLANE // rs
    assert bs % pack == 0
    t_len = s_len + N_META
    tp = -(-t_len // QB) * QB
    n_pages = page_table.shape[1]
    past = n_pages * PAGE
    topk_p = min(TOPK_MAX, s_len // 4)
    topk_s = min(TOPK_MAX, (past + ts) // 4)
    n_prompt = nb * tp
    n_tok = n_prompt + bs * ts
    n_pad = -(-n_tok // ROW_TILE) * ROW_TILE
    tm_proj = 512 if n_pad % 512 == 0 else ROW_TILE

    hp = jnp.concatenate([jnp.broadcast_to(meta_tokens.astype(x_prompt.dtype)[None], (nb, N_META, d)),
                          x_prompt, jnp.zeros((nb, tp - t_len, d), x_prompt.dtype)], axis=1)
    h0 = jnp.concatenate([hp.reshape(n_prompt, d), x_sample.reshape(bs * ts, d),
                          jnp.zeros((n_pad - n_tok, d), x_prompt.dtype)], axis=0)
    pos = jnp.concatenate([jnp.tile(jnp.arange(tp, dtype=i32), nb),
                           jnp.tile(past + jnp.arange(ts, dtype=i32), bs),
                           jnp.zeros((n_pad - n_tok,), i32)])
    cos_t, sin_t = _rope_tables(pos)

    l = 0
    w_pack = _pack_w_in(w_in[l], d)
    wuq_p = _pack_w_uq(w_uq[l])
    wuk_t = jnp.transpose(w_uk[l], (1, 2, 0)).astype(bf16)
    wuv_p = jnp.transpose(w_uv[l], (1, 0, 2)).astype(bf16)
    wr = jnp.pad(jnp.concatenate([w_rg[l], w_re[l]], axis=1),
                 ((0, 0), (0, LANE - N_GROUPS - N_EXPERTS))).astype(bf16)

    proj = _norm_matmul(h0, g_attn[l], w_pack, tm_proj, 1024)
    qcat, ckv, kr, kcat, kab, vab, kib = _post_project(proj, cos_t, sin_t, g_q[l], g_kv[l], wuq_p, wuk_t)

    bias_tab = rel_bias[_rel_bucket(jnp.arange(MAX_DIST + 1, dtype=i32))].astype(f32)
    span = QB + BAND
    wvec = bias_tab[jnp.clip(jnp.arange(span, dtype=i32) - (QB - 1), 0, MAX_DIST)].T
    hank = jnp.tile(wvec, (1, QB + 1))[:, :QB * (span + 1)].reshape(H_A, QB, span + 1)
    band = hank[:, :, :BAND][:, :, ::-1]
    far1 = bias_tab[MAX_DIST]
    far = jnp.broadcast_to(far1[:, None, None], (H_A, 1, LANE))
    oa_p = _dsa_prompt(proj, kib, kab, vab, band, far, nb, tp, topk_p)
    ob_p = _mla_prompt(qcat, kcat, nb, tp)

    pp = _pages_per_chunk(n_pages)
    ch = pp * PAGE
    nbp = bs // pack
    srow = lambda a: a[n_prompt:n_prompt + bs * ts]
    seg = lambda name: srow(proj)[:, _COL[name][0] * _COL[name][1]:(_COL[name][0] + 1) * _COL[name][1]]

    def pack_rows(a, lead):
        feat = a.shape[1:]
        a = a.reshape((nbp, pack, ts) + feat)
        a = jnp.pad(a, ((0, 0), (0, 0), (0, rs - ts)) + ((0, 0),) * len(feat))
        nl = len(lead)
        a = jnp.transpose(a, (0,) + tuple(range(3, 3 + nl)) + (1, 2, 3 + nl))
        return a.reshape((nbp,) + lead + (SUBLANE, feat[-1]))

    qi_s = pack_rows(seg("qi").reshape(bs * ts, H_IDX, D_IDX), (H_IDX,)).reshape(nbp, H_IDX * SUBLANE, D_IDX)
    wi_s = pack_rows(seg("wi")[:, :H_IDX].reshape(bs * ts, H_IDX, 1), (H_IDX,)).reshape(nbp, H_IDX * SUBLANE, 1)
    qa_s = pack_rows(seg("qa").reshape(bs * ts, KVH_A, REP_A, DH_A), (KVH_A, REP_A))
    qa_s = qa_s.reshape(nbp, KVH_A, REP_A * SUBLANE, DH_A)
    nblk_s = -(-(bs * ts) // QB)
    qc_s = qcat[n_prompt // QB:n_prompt // QB + nblk_s].transpose(0, 2, 1, 3)
    qc_s = qc_s.reshape(nblk_s * QB, H_B, MLA_KCAT)[:bs * ts].reshape(bs, ts, H_B, MLA_KCAT)
    qc_s = jnp.pad(qc_s, ((0, 0), (0, SUBLANE - ts), (0, 0), (0, 0))).transpose(0, 2, 1, 3)
    qc_s = qc_s.reshape(bs, H_B * SUBLANE, MLA_KCAT)
    padp = lambda a: jnp.pad(a.reshape(bs, ts, a.shape[-1]), ((0, 0), (0, PAGE - ts), (0, 0)))
    ki_new = padp(srow(kib)[:, :D_IDX])
    kc_new = padp(srow(kcat))
    ka_new = padp(srow(kab))
    va_new = padp(srow(vab))
    idx_t = jnp.swapaxes(cache_idx_k, 2, 3)
    kr_t = jnp.swapaxes(cache_kr, 2, 3)
    scores, ob_s = _sample_pass_a(page_table, qi_s, wi_s * IDX_W_SCALE, qc_s, ki_new, kc_new,
                                  idx_t, cache_ckv, kr_t, ts, rs, pp)
    thr = _sample_thresholds(scores, topk_s)
    j8 = (jnp.arange(SUBLANE, dtype=i32) % rs)[:, None]
    near = min(ch, BAND)
    bias_last = _bias_of_dist(bias_tab, j8 + near - jnp.arange(near, dtype=i32)[None, :])
    if ch > near:
        bias_last = jnp.concatenate(
            [jnp.broadcast_to(far1[:, None, None], (H_A, SUBLANE, ch - near)), bias_last], axis=2)
    bias_new = _bias_of_dist(bias_tab, j8 - jnp.arange(PAGE, dtype=i32)[None, :])
    shp = lambda a: a.reshape((KVH_A, REP_A) + a.shape[1:])
    oa_s = _sample_pass_b(page_table, scores, thr, qa_s, ka_new, va_new, shp(bias_last), shp(bias_new),
                          shp(far1[:, None, None]), cache_k.reshape(-1, PAGE * KVH_A, DH_A),
                          cache_v.reshape(-1, PAGE * KVH_A, DH_A), rs, pp)
    oa_s = oa_s.reshape(nbp, pack, KVH_A, REP_A, pack, rs, DH_A)
    oa_s = jnp.moveaxis(jnp.diagonal(oa_s, axis1=1, axis2=4), -1, 1)
    oa_s = oa_s[:, :, :, :, :ts].transpose(0, 1, 4, 2, 3, 5).reshape(bs * ts, H_A * DH_A).astype(bf16)
    ob_s = ob_s.reshape(bs, H_B, SUBLANE, KV_LORA)[:, :, :ts].transpose(0, 2, 1, 3)
    ob_s = ob_s.reshape(bs * ts, H_B * KV_LORA).astype(bf16)
    tail = lambda w: jnp.zeros((n_pad - n_tok, w), bf16)
    oa = jnp.concatenate([oa_p, oa_s, tail(H_A * DH_A)], axis=0)
    ob = jnp.concatenate([ob_p, ob_s, tail(H_B * KV_LORA)], axis=0)

    h1, xn3, route = _merge(oa, ob, proj, h0, w_oa[l].astype(bf16), wuv_p, w_ob[l].astype(bf16),
                            w_o[l].astype(bf16), g_ffn[l], wr)
    eid = route[:, 0:TOP_E].astype(i32)
    wts = route[:, TOP_E:2 * TOP_E]
    tile_expert, src_token, n_used, wt_sorted, pos0, pos1 = _sort_slots(eid, wts, ROW_TILE)
    y_sorted = _moe_experts(tile_expert, src_token, n_used, xn3, wt_sorted, w_gate[l], w_up[l], w_down[l],
                            ROW_TILE)
    y = _combine(pos0, pos1, h1, g_final, y_sorted)

    y_prompt = y[:n_prompt].reshape(nb, tp, d)[:, N_META:t_len]
    y_sample = y[n_prompt:n_tok].reshape(bs, ts, d)
    seg_all = lambda name: proj[:, _COL[name][0] * _COL[name][1]:(_COL[name][0] + 1) * _COL[name][1]]

    def states(rows, lead):
        ka = rows(seg_all("ka")).reshape(lead + (KVH_A, DH_A))
        va = rows(seg_all("va")).reshape(lead + (KVH_A, DH_A))
        ki = rows(seg_all("ki"))[..., :D_IDX].reshape(lead + (D_IDX,))
        cc = rows(ckv).reshape(lead + (KV_LORA,))
        rr = rows(kr)[..., :D_ROPE].reshape(lead + (D_ROPE,))
        return [a[None] for a in (ka, va, ki, cc, rr)]

    prow = lambda a: a[:n_prompt].reshape(nb, tp, a.shape[-1])[:, :t_len]
    st_p = states(prow, (nb, t_len))
    st_s = states(lambda a: a[n_prompt:n_tok], (bs, ts))
    return (y_prompt, y_sample, *st_p, *st_s)
```

```python
import functools
import math

import numpy as np
import jax
import jax.numpy as jnp
from jax import lax
from jax.experimental import pallas as pl
from jax.experimental.pallas import tpu as pltpu

N_META = 16
H_A, KVH_A, DH_A = 8, 2, 128
REP_A = H_A // KVH_A
H_IDX, D_IDX = 16, 64
TOPK_MAX = 256
IDX_W_SCALE = (H_IDX ** -0.5) * (D_IDX ** -0.5)
N_BUCKETS, MAX_DIST = 32, 128
H_B, Q_LORA, KV_LORA, D_NOPE, D_ROPE, D_V = 8, 512, 256, 128, 64, 128
ROPE_THETA = 10000.0
MLA_SCALE = (D_NOPE + D_ROPE) ** -0.5
DSA_SCALE = DH_A ** -0.5
N_GROUPS, E_PER_GROUP, TOP_E, D_EXPERT = 4, 8, 2, 512
N_EXPERTS = N_GROUPS * E_PER_GROUP
EPS = 1e-6
PAGE = 128

LANE = 128
SUBLANE = 8
VMEM_LIMIT_BYTES = 56 * 1024 * 1024

QB = 128
ROW_TILE = 256
MLA_KCAT = KV_LORA + LANE
BAND = 2 * QB
INT_MIN = -2 ** 31
NEG_INF_KEY = -2 ** 31 + 0x7FFFFF

_SEGS = (("ga", 2048), ("gb", 2048), ("qa", 1024), ("qi", 1024), ("dq", 512), ("ka", 256),
         ("va", 256), ("dkv", 256), ("wi", 128), ("ki", 128), ("krr", 128), ("krs", 128))
_COL = {}
_o = 0
for _n, _w in _SEGS:
    assert _o % _w == 0
    _COL[_n] = (_o // _w, _w)
    _o += _w
PROJ_W = 8192
assert _o <= PROJ_W

f32, bf16, i32 = jnp.float32, jnp.bfloat16, jnp.int32


def _cparams(*sem):
    return pltpu.CompilerParams(dimension_semantics=sem, vmem_limit_bytes=VMEM_LIMIT_BYTES)


def _dot(a, b):
    return jnp.dot(a, b, preferred_element_type=f32)


def _dot_nt(a, b):
    return lax.dot_general(a, b, (((1,), (1,)), ((), ())), preferred_element_type=f32)


def _rms(x, g):
    return x * lax.rsqrt(jnp.mean(x * x, axis=-1, keepdims=True) + EPS) * g


def _const_spec(shape):
    nd = len(shape)
    return pl.BlockSpec(shape, lambda *_: (0,) * nd, pipeline_mode=pl.Buffered(1))


def _split_rows(x):
    return [x[:, j * LANE:(j + 1) * LANE] for j in range(x.shape[1] // LANE)]


def _norm_matmul_kernel(x_ref, g_ref, w_ref, o_ref, xn_ref):
    @pl.when(pl.program_id(1) == 0)
    def _():
        xn_ref[...] = _rms(x_ref[...], g_ref[...]).astype(bf16)

    o_ref[...] = _dot(xn_ref[...], w_ref[...])


def _norm_matmul(x, g, w, tm, tn):
    n, k = x.shape
    m = w.shape[1]
    return pl.pallas_call(
        _norm_matmul_kernel,
        grid=(n // tm, m // tn),
        in_specs=[pl.BlockSpec((tm, k), lambda i, j: (i, 0)),
                  pl.BlockSpec((1, k), lambda i, j: (0, 0)),
                  pl.BlockSpec((k, tn), lambda i, j: (0, j))],
        out_specs=pl.BlockSpec((tm, tn), lambda i, j: (i, j)),
        out_shape=jax.ShapeDtypeStruct((n, m), f32),
        scratch_shapes=[pltpu.VMEM((tm, k), bf16)],
        compiler_params=_cparams("parallel", "arbitrary"),
        name="in_proj",
    )(x, g.reshape(1, k), w)


def _post_kernel(dq_ref, dkv_ref, krr_ref, krs_ref, ka_ref, va_ref, ki_ref, cos_ref, sin_ref,
                 gq_ref, gkv_ref, wuq_ref, wuk_ref,
                 qcat_ref, ckv_ref, kr_ref, kcat_ref, kab_ref, vab_ref, kib_ref):
    tm = dq_ref.shape[0]
    cos_t = cos_ref[...]
    sin_t = sin_ref[...]
    dqn = _rms(dq_ref[...], gq_ref[...]).astype(bf16)
    q = _dot(dqn, wuq_ref[...])
    for h in range(H_B):
        qlat = _dot(q[:, h * 128:(h + 1) * 128].astype(bf16), wuk_ref[h])
        qr = (q[:, 1024 + h * 128:1024 + (h + 1) * 128] * cos_t
              + q[:, 2048 + h * 128:2048 + (h + 1) * 128] * sin_t)
        for sb in range(tm // QB):
            qcat_ref[sb, h, :, 0:KV_LORA] = qlat[sb * QB:(sb + 1) * QB].astype(bf16)
            qcat_ref[sb, h, :, KV_LORA:MLA_KCAT] = qr[sb * QB:(sb + 1) * QB].astype(bf16)
    ckv = _rms(dkv_ref[...], gkv_ref[...])
    ckv_ref[...] = ckv
    kr = krr_ref[...] * cos_t + krs_ref[...] * sin_t
    kr_ref[...] = kr
    kcat_ref[:, 0:KV_LORA] = ckv.astype(bf16)
    kcat_ref[:, KV_LORA:MLA_KCAT] = kr.astype(bf16)
    kab_ref[...] = ka_ref[...].astype(bf16)
    vab_ref[...] = va_ref[...].astype(bf16)
    kib_ref[...] = ki_ref[...].astype(bf16)


def _post_project(proj, cos_t, sin_t, g_q, g_kv, wuq_p, wuk_t):
    n = proj.shape[0]
    tm = ROW_TILE

    def col(name):
        j, w = _COL[name]
        return pl.BlockSpec((tm, w), lambda i, j=j: (i, j))

    row = lambda w: pl.BlockSpec((tm, w), lambda i: (i, 0))
    outs = pl.pallas_call(
        _post_kernel,
        grid=(n // tm,),
        in_specs=[col("dq"), col("dkv"), col("krr"), col("krs"), col("ka"), col("va"), col("ki"),
                  row(LANE), row(LANE),
                  _const_spec((1, Q_LORA)), _const_spec((1, KV_LORA)),
                  _const_spec(wuq_p.shape), _const_spec(wuk_t.shape)],
        out_specs=[pl.BlockSpec((tm // QB, H_B, QB, MLA_KCAT), lambda i: (i, 0, 0, 0)),
                   row(KV_LORA), row(LANE), row(MLA_KCAT), row(256), row(256), row(LANE)],
        out_shape=[jax.ShapeDtypeStruct((n // QB, H_B, QB, MLA_KCAT), bf16),
                   jax.ShapeDtypeStruct((n, KV_LORA), f32),
                   jax.ShapeDtypeStruct((n, LANE), f32),
                   jax.ShapeDtypeStruct((n, MLA_KCAT), bf16),
                   jax.ShapeDtypeStruct((n, 256), bf16),
                   jax.ShapeDtypeStruct((n, 256), bf16),
                   jax.ShapeDtypeStruct((n, LANE), bf16)],
        compiler_params=_cparams("parallel"),
        name="post_proj",
    )(proj, proj, proj, proj, proj, proj, proj, cos_t, sin_t,
      g_q.reshape(1, -1), g_kv.reshape(1, -1), wuq_p, wuk_t)
    return outs


def _mla_prompt_kernel(q_ref, k_ref, o_ref, *, tk):
    i = pl.program_id(1)
    tp = k_ref.shape[0]
    t0 = i * QB
    rows = H_B * QB
    q = q_ref[0].reshape(rows, MLA_KCAT)
    qpos = t0 + (lax.broadcasted_iota(i32, (rows, 1), 0) & (QB - 1))
    nk = (t0 + QB + tk - 1) // tk

    def body(c, carry):
        m, l, acc = carry
        lo = c * tk
        st = pl.multiple_of(jnp.minimum(lo, tp - tk), QB)
        ks = k_ref[pl.ds(st, tk), :]
        s = _dot_nt(q, ks) * MLA_SCALE
        kpos = st + lax.broadcasted_iota(i32, (1, tk), 1)
        s = jnp.where((kpos <= qpos) & (kpos >= lo), s, -jnp.inf)
        m_new = jnp.maximum(m, jnp.max(s, axis=1, keepdims=True))
        p = jnp.exp(s - m_new)
        alpha = jnp.exp(m - m_new)
        l = alpha * l + jnp.sum(p, axis=1, keepdims=True)
        acc = alpha * acc + _dot(p.astype(bf16), ks[:, 0:KV_LORA])
        return m_new, l, acc

    m0 = jnp.full((rows, 1), -jnp.inf, f32)
    l0 = jnp.zeros((rows, 1), f32)
    a0 = jnp.zeros((rows, KV_LORA), f32)
    _, l, acc = lax.fori_loop(0, nk, body, (m0, l0, a0))
    o = acc / l
    for h in range(H_B):
        o_ref[:, h * KV_LORA:(h + 1) * KV_LORA] = o[h * QB:(h + 1) * QB].astype(bf16)


def _mla_prompt(qcat, kcat, nb, tp):
    nq = tp // QB
    tk = min(512, tp)
    return pl.pallas_call(
        functools.partial(_mla_prompt_kernel, tk=tk),
        grid=(nb, nq),
        in_specs=[pl.BlockSpec((1, H_B, QB, MLA_KCAT), lambda b, i: (b * nq + i, 0, 0, 0)),
                  pl.BlockSpec((tp, MLA_KCAT), lambda b, i: (b, 0))],
        out_specs=pl.BlockSpec((QB, H_B * KV_LORA), lambda b, i: (b * nq + i, 0)),
        out_shape=jax.ShapeDtypeStruct((nb * tp, H_B * KV_LORA), bf16),
        compiler_params=_cparams("parallel", "arbitrary"),
        name="mla_prompt",
    )(qcat, kcat)


def _sortable(score):
    b = lax.bitcast_convert_type(score, i32)
    return jnp.where(b < 0, b ^ jnp.int32(0x7FFFFFFF), b)


def _kth_largest(key_ref, k, count, shape1, two_bits=False):
    def n_ge(cand_u):
        return count((key_ref[...] >= (cand_u ^ jnp.int32(INT_MIN))).astype(i32))

    def body(it, res_u):
        cand_u = res_u | lax.shift_left(jnp.int32(1), 31 - it)
        return jnp.where(n_ge(cand_u) >= k, cand_u, res_u)

    def body2(it, res_u):
        c_hi = res_u | lax.shift_left(jnp.int32(1), 31 - 2 * it)
        c_lo = res_u | lax.shift_left(jnp.int32(1), 30 - 2 * it)
        c_both = c_hi | c_lo
        best = jnp.where(n_ge(c_lo) >= k, c_lo, res_u)
        best = jnp.where(n_ge(c_hi) >= k, c_hi, best)
        return jnp.where(n_ge(c_both) >= k, c_both, best)

    if two_bits:
        res_u = lax.fori_loop(0, 16, body2, jnp.zeros(shape1, i32), unroll=2)
    else:
        res_u = lax.fori_loop(0, 32, body, jnp.zeros(shape1, i32), unroll=4)
    return res_u ^ jnp.int32(INT_MIN)


def _online_update(carry, s, v):
    m, l, acc = carry
    m_new = jnp.maximum(m, jnp.max(s, axis=1, keepdims=True))
    m_safe = jnp.where(m_new == -jnp.inf, 0.0, m_new)
    p = jnp.exp(s - m_safe)
    alpha = jnp.exp(m - m_safe)
    l = alpha * l + jnp.sum(p, axis=1, keepdims=True)
    acc = alpha * acc + _dot(p.astype(bf16), v)
    return m_new, l, acc


def _online_update_t(carry, s, vt):
    m, l, acc = carry
    m_new = jnp.maximum(m, jnp.max(s, axis=0, keepdims=True))
    m_safe = jnp.where(m_new == -jnp.inf, 0.0, m_new)
    p = jnp.exp(s - m_safe)
    alpha = jnp.exp(m - m_safe)
    l = alpha * l + jnp.sum(p, axis=0, keepdims=True)
    acc = alpha * acc + _dot(vt, p.astype(bf16))
    return m_new, l, acc


def _dsa_prompt_kernel(qa_ref, qi_ref, wi_ref, ki_ref, ka_ref, vt_ref, band_ref, far_ref, o_ref,
                       key_ref, qs_ref, qg_ref, thr_ref, *, topk):
    i = pl.program_id(1)
    t0 = i * QB
    tp = ki_ref.shape[0]
    nck = tp // QB
    wide = 2 * QB
    qpos = t0 + lax.broadcasted_iota(i32, (1, QB), 1)

    qi = qi_ref[...]
    for h in range(H_IDX):
        qs_ref[h * QB:(h + 1) * QB, :] = qi[:, h * D_IDX:(h + 1) * D_IDX].astype(bf16)
    qa = qa_ref[...]
    for h in range(H_A):
        qg_ref[h * QB:(h + 1) * QB, :] = qa[:, h * DH_A:(h + 1) * DH_A].astype(bf16)
    wt = (wi_ref[...] * IDX_W_SCALE).T

    def score_chunk(c, carry):
        st = pl.multiple_of(jnp.minimum(c * wide, tp - wide), QB)
        s_all = _dot_nt(ki_ref[pl.ds(st, wide), 0:D_IDX], qs_ref[...])
        score = jnp.zeros((wide, QB), f32)
        for h in range(H_IDX):
            score = score + wt[h:h + 1, :] * jnp.maximum(s_all[:, h * QB:(h + 1) * QB], 0.0)
        kpos = st + lax.broadcasted_iota(i32, (wide, 1), 0)
        key = jnp.where(kpos <= qpos, _sortable(score), jnp.int32(NEG_INF_KEY))
        cj = st // QB
        key_ref[cj] = key[0:QB]
        key_ref[cj + 1] = key[QB:wide]
        return carry

    lax.fori_loop(0, (i + 2) // 2, score_chunk, 0)

    def blank_chunk(c, carry):
        key_ref[c] = jnp.full((QB, QB), NEG_INF_KEY, i32)
        return carry

    lax.fori_loop(i + 1, nck, blank_chunk, 0)

    def count_keys(ge):
        per_row = jnp.sum(ge, axis=0)
        per_group = jnp.sum(per_row.reshape(QB // SUBLANE, SUBLANE, QB), axis=0)
        return jnp.sum(per_group, axis=0, keepdims=True)[None]

    lo = 0
    for hi in sorted({-(-nck * part // 3) for part in (1, 2, 3)}):
        @pl.when((i >= lo) & (i < hi))
        def _():
            thr_ref[...] = _kth_largest(key_ref.at[0:hi], topk, count_keys, (1, 1, QB), two_bits=True)

        lo = hi
    thr = thr_ref[0]

    gcols = REP_A * QB

    def attend(state, st, cj, nk, sel, bias_of):
        vt = vt_ref[pl.ds(cj, nk)]
        vt = vt[0] if nk == 1 else jnp.concatenate([vt[n] for n in range(nk)], axis=1)
        new = []
        for g in range(KVH_A):
            cs = slice(g * gcols, (g + 1) * gcols)
            lg = _dot_nt(ka_ref[pl.ds(st, nk * QB), g * DH_A:(g + 1) * DH_A], qg_ref[cs, :]) * DSA_SCALE
            x = jnp.concatenate(
                [jnp.where(sel, lg[:, r * QB:(r + 1) * QB] + bias_of(REP_A * g + r), -jnp.inf)
                 for r in range(REP_A)], axis=1)
            new.append(_online_update_t(state[g], x, vt[g * DH_A:(g + 1) * DH_A]))
        return tuple(new)

    def far_chunk(c, state):
        st = pl.multiple_of(c * wide, wide)
        kk = key_ref[pl.ds(2 * c, 2)]
        key = jnp.concatenate([kk[0], kk[1]], axis=0)
        kpos = st + lax.broadcasted_iota(i32, (wide, 1), 0)
        return attend(state, st, 2 * c, 2, (key >= thr) & (kpos < t0 - QB), lambda h: far_ref[h][:, 0:1])

    init = (jnp.full((1, gcols), -jnp.inf, f32), jnp.zeros((1, gcols), f32), jnp.zeros((DH_A, gcols), f32))
    state = lax.fori_loop(0, i // 2, far_chunk, (init,) * KVH_A)
    ca = jnp.maximum(i - 1, 0)
    st_a = pl.multiple_of(ca * QB, QB)
    krow = lax.broadcasted_iota(i32, (QB, 1), 0)
    state = attend(state, st_a, ca, 1, (key_ref[ca] >= thr) & (st_a + krow < t0),
                   lambda h: band_ref[h][0:QB])
    state = attend(state, pl.multiple_of(t0, QB), i, 1, (key_ref[i] >= thr) & (t0 + krow <= qpos),
                   lambda h: band_ref[h][QB:BAND])
    for g in range(KVH_A):
        _, l, acc = state[g]
        o = acc / l
        for r in range(REP_A):
            h = REP_A * g + r
            o_ref[h * DH_A:(h + 1) * DH_A, :] = o[:, r * QB:(r + 1) * QB].astype(bf16)


def _dsa_prompt(proj, kib, kab, vab, band, far, nb, tp, topk):
    assert tp >= 2 * QB
    nq = tp // QB
    jqa, jqi, jwi = _COL["qa"][0], _COL["qi"][0], _COL["wi"][0]
    vt = vab[:nb * tp].reshape(nb, nq, QB, KVH_A * DH_A).transpose(0, 1, 3, 2)
    band_t = band.transpose(0, 2, 1)
    return pl.pallas_call(
        functools.partial(_dsa_prompt_kernel, topk=topk),
        grid=(nb, nq),
        in_specs=[pl.BlockSpec((QB, 1024), lambda b, i: (b * nq + i, jqa)),
                  pl.BlockSpec((QB, 1024), lambda b, i: (b * nq + i, jqi)),
                  pl.BlockSpec((QB, LANE), lambda b, i: (b * nq + i, jwi)),
                  pl.BlockSpec((tp, LANE), lambda b, i: (b, 0)),
                  pl.BlockSpec((tp, 256), lambda b, i: (b, 0)),
                  pl.BlockSpec((None, nq, KVH_A * DH_A, QB), lambda b, i: (b, 0, 0, 0)),
                  _const_spec((H_A, BAND, QB)),
                  _const_spec((H_A, 1, LANE))],
        out_specs=pl.BlockSpec((None, H_A * DH_A, QB), lambda b, i: (b, 0, i)),
        out_shape=jax.ShapeDtypeStruct((nb, H_A * DH_A, tp), bf16),
        scratch_shapes=[pltpu.VMEM((tp // QB, QB, QB), i32),
                        pltpu.VMEM((H_IDX * QB, D_IDX), bf16),
                        pltpu.VMEM((H_A * QB, DH_A), bf16),
                        pltpu.VMEM((1, 1, QB), i32)],
        compiler_params=_cparams("parallel", "arbitrary"),
        name="dsa_prompt",
    )(proj, proj, proj, kib, kab, vt, band_t, far)


def _page_copies(pt_ref, base, c, slot, pp, srcs, bufs, sems):
    out = []
    for p in range(pp):
        page = pt_ref[base + c * pp + p]
        for a, (src, buf) in enumerate(zip(srcs, bufs)):
            out.append(pltpu.make_async_copy(src(page), buf(slot, p), sems.at[slot, a]))
    return out


def _page_stream(pt_ref, n_pages, pp, srcs, bufs, sems):
    s_id = pl.program_id(0)
    n_seq = pl.num_programs(0)
    n_chunks = n_pages // pp

    def copies(seq, c, slot):
        return _page_copies(pt_ref, seq * n_pages, c, slot, pp, srcs, bufs, sems)

    def prologue():
        @pl.when(s_id == 0)
        def _():
            for cp in copies(0, 0, 0):
                cp.start()

    def advance(c):
        slot = (s_id * n_chunks + c) % 2

        @pl.when(c + 1 < n_chunks)
        def _():
            for cp in copies(s_id, c + 1, 1 - slot):
                cp.start()

        @pl.when((c + 1 == n_chunks) & (s_id + 1 < n_seq))
        def _():
            for cp in copies(s_id + 1, 0, 1 - slot):
                cp.start()

        for cp in copies(s_id, c, slot):
            cp.wait()
        return slot

    return prologue, advance


def _merge_softmax(a, b):
    (ma, la, acca), (mb, lb, accb) = a, b
    m = jnp.maximum(ma, mb)
    m_safe = jnp.where(m == -jnp.inf, 0.0, m)
    ea = jnp.exp(ma - m_safe)
    eb = jnp.exp(mb - m_safe)
    return m, la * ea + lb * eb, acca * ea + accb * eb


def _sample_a_kernel(pt_ref, qi_ref, wi_ref, qc_ref, kin_ref, kcn_ref, idx_hbm, ckv_hbm, kr_hbm,
                     sc_ref, ob_ref, ibuf, cbuf, rbuf, sems, *, n_pages, pp, ts, rs):
    s_id = pl.program_id(0)
    pack = SUBLANE // rs
    par = s_id % pack
    n_chunks = n_pages // pp
    ch = pp * PAGE
    srcs = (lambda pg: idx_hbm.at[0, pg], lambda pg: ckv_hbm.at[0, pg], lambda pg: kr_hbm.at[0, pg])
    bufs = (lambda sl, p: ibuf.at[sl, :, pl.ds(p * PAGE, PAGE)],
            lambda sl, p: cbuf.at[sl, pl.ds(p * PAGE, PAGE)],
            lambda sl, p: rbuf.at[sl, :, pl.ds(p * PAGE, PAGE)])
    prologue, advance = _page_stream(pt_ref, n_pages, pp, srcs, bufs, sems)

    qi = qi_ref[0].astype(bf16)
    w = wi_ref[0]
    qlat = qc_ref[0][:, 0:KV_LORA]
    qrope = qc_ref[0][:, KV_LORA:KV_LORA + D_ROPE]
    rows = H_B * SUBLANE

    def head_sum(s):
        s = jnp.maximum(s, 0.0) * w
        return jnp.sum(s.reshape(H_IDX, SUBLANE, s.shape[1]), axis=0)

    def put_scores(c, sc):
        for q in range(pack):
            @pl.when(par == q)
            def _():
                sc_ref[0, c, q * rs:(q + 1) * rs, :] = sc[q * rs:(q + 1) * rs]

    prologue()
    nsplit = 2 if ch >= 2 * LANE else 1
    part = ch // nsplit

    def body(c, carry):
        slot = advance(c)
        put_scores(c, head_sum(_dot(qi, ibuf[slot].astype(bf16))))
        new = []
        for k in range(nsplit):
            ck = cbuf[slot, k * part:(k + 1) * part].astype(bf16)
            rk = rbuf[slot, :, k * part:(k + 1) * part].astype(bf16)
            s = (_dot_nt(qlat, ck) + _dot(qrope, rk)) * MLA_SCALE
            new.append(_online_update(carry[k], s, ck))
        return tuple(new)

    init = (jnp.full((rows, 1), -jnp.inf, f32), jnp.zeros((rows, 1), f32),
            jnp.zeros((rows, KV_LORA), f32))
    parts = lax.fori_loop(0, n_chunks, body, (init,) * nsplit)
    carry = parts[0] if nsplit == 1 else _merge_softmax(parts[0], parts[1])

    kj = lax.broadcasted_iota(i32, (1, PAGE), 1)
    ok8 = (kj <= (lax.broadcasted_iota(i32, (SUBLANE, 1), 0) & (rs - 1))) & (kj < ts)
    sc_new = jnp.where(ok8, head_sum(_dot_nt(qi, kin_ref[0])), -jnp.inf)
    if ch > PAGE:
        sc_new = jnp.concatenate([sc_new, jnp.full((SUBLANE, ch - PAGE), -jnp.inf, f32)], axis=1)
    put_scores(n_chunks, sc_new)
    kcn = kcn_ref[0]
    okr = (kj <= (lax.broadcasted_iota(i32, (rows, 1), 0) & (SUBLANE - 1))) & (kj < ts)
    s = jnp.where(okr, _dot_nt(qc_ref[0], kcn) * MLA_SCALE, -jnp.inf)
    _, l, acc = _online_update(carry, s, kcn[:, 0:KV_LORA])
    ob_ref[0] = acc / l


def _sample_pass_a(page_table, qi_s, wi_s, qc_s, ki_new, kc_new, idx_t, cache_ckv, kr_t, ts, rs, pp):
    bs, n_pages = page_table.shape
    pack = SUBLANE // rs
    n_chunks = n_pages // pp
    ch = pp * PAGE
    blk = lambda *shape: pl.BlockSpec((1,) + shape, lambda s, pt: (s,) + (0,) * len(shape))
    pblk = lambda *shape: pl.BlockSpec((1,) + shape, lambda s, pt: (s // pack,) + (0,) * len(shape))
    any_spec = pl.BlockSpec(memory_space=pl.ANY)
    grid_spec = pltpu.PrefetchScalarGridSpec(
        num_scalar_prefetch=1,
        grid=(bs,),
        in_specs=[pblk(H_IDX * SUBLANE, D_IDX), pblk(H_IDX * SUBLANE, 1), blk(H_B * SUBLANE, MLA_KCAT),
                  blk(PAGE, D_IDX), blk(PAGE, MLA_KCAT), any_spec, any_spec, any_spec],
        out_specs=[pblk(n_chunks + 1, SUBLANE, ch), blk(H_B * SUBLANE, KV_LORA)],
        scratch_shapes=[pltpu.VMEM((2, D_IDX, ch), f32), pltpu.VMEM((2, ch, KV_LORA), f32),
                        pltpu.VMEM((2, D_ROPE, ch), f32), pltpu.SemaphoreType.DMA((2, 3))])
    return pl.pallas_call(
        functools.partial(_sample_a_kernel, n_pages=n_pages, pp=pp, ts=ts, rs=rs),
        grid_spec=grid_spec,
        out_shape=[jax.ShapeDtypeStruct((bs // pack, n_chunks + 1, SUBLANE, ch), f32),
                   jax.ShapeDtypeStruct((bs, H_B * SUBLANE, KV_LORA), f32)],
        compiler_params=_cparams("arbitrary"),
        name="sample_idx_mla",
    )(page_table.reshape(-1), qi_s, wi_s, qc_s, ki_new, kc_new, idx_t, cache_ckv, kr_t)


def _thr_kernel(sc_ref, thr_ref, key_ref, *, topk):
    g = sc_ref.shape[0]
    key_ref[...] = _sortable(sc_ref[...])
    thr = _kth_largest(key_ref, topk, lambda ge: jnp.sum(ge, axis=(1, 3), keepdims=True), (g, 1, SUBLANE, 1))
    thr_ref[...] = jnp.broadcast_to(thr[:, 0], (g, SUBLANE, LANE))


def _sample_thresholds(scores, topk):
    nblk, nc1, _, ch = scores.shape
    g = 8
    while nblk % g:
        g //= 2
    return pl.pallas_call(
        functools.partial(_thr_kernel, topk=topk),
        grid=(nblk // g,),
        in_specs=[pl.BlockSpec((g, nc1, SUBLANE, ch), lambda i: (i, 0, 0, 0))],
        out_specs=pl.BlockSpec((g, SUBLANE, LANE), lambda i: (i, 0, 0)),
        out_shape=jax.ShapeDtypeStruct((nblk, SUBLANE, LANE), i32),
        scratch_shapes=[pltpu.VMEM((g, nc1, SUBLANE, ch), i32)],
        compiler_params=_cparams("parallel"),
        name="sample_topk_thr",
    )(scores)


def _sample_b_kernel(pt_ref, sc_ref, thr_ref, qa_ref, kn_ref, vn_ref, bl_ref, bn_ref, far_ref, k_hbm, v_hbm,
                     o_ref, kbuf, vbuf, sems, *, n_pages, pp):
    n_chunks = n_pages // pp
    ch = pp * PAGE
    prow = PAGE * KVH_A
    srcs = (lambda pg: k_hbm.at[pg], lambda pg: v_hbm.at[pg])
    bufs = (lambda sl, p: kbuf.at[sl, pl.ds(p * prow, prow)], lambda sl, p: vbuf.at[sl, pl.ds(p * prow, prow)])
    prologue, advance = _page_stream(pt_ref, n_pages, pp, srcs, bufs, sems)
    prologue()
    thr = thr_ref[0][:, 0:1]
    rows = REP_A * SUBLANE
    qa = [qa_ref[0, g].astype(bf16) for g in range(KVH_A)]
    nsplit = 2 if ch >= 2 * LANE else 1
    part = ch // nsplit

    def attend(carry, sc, kk, vv, bias_of):
        sel = _sortable(sc) >= thr
        n = sel.shape[1]
        new = []
        for g in range(KVH_A):
            lg = _dot_nt(qa[g], kk(g)) * DSA_SCALE
            lg = lg.reshape(REP_A, SUBLANE, n) + bias_of(g)
            lg = jnp.where(sel[None], lg, -jnp.inf).reshape(rows, n)
            new.append(_online_update(carry[g], lg, vv(g)))
        return tuple(new)

    def body(c, carry):
        slot = advance(c)
        is_last = jnp.broadcast_to(c, (REP_A, SUBLANE, part)) == n_chunks - 1
        new = []
        for k in range(nsplit):
            lo = k * part
            head_rows = lambda buf, g: buf[slot, pl.ds(KVH_A * lo + g, part, stride=KVH_A), :].astype(bf16)
            new.append(attend(
                carry[k], sc_ref[0, c][:, lo:lo + part],
                lambda g: head_rows(kbuf, g), lambda g: head_rows(vbuf, g),
                lambda g: jnp.where(is_last, bl_ref[g][:, :, lo:lo + part], far_ref[g])))
        return tuple(new)

    init = (jnp.full((rows, 1), -jnp.inf, f32), jnp.zeros((rows, 1), f32), jnp.zeros((rows, DH_A), f32))
    parts = lax.fori_loop(0, n_chunks, body, ((init,) * KVH_A,) * nsplit)
    sel_new = _sortable(sc_ref[0, n_chunks][:, 0:PAGE]) >= thr
    for g in range(KVH_A):
        carry = parts[0][g] if nsplit == 1 else _merge_softmax(parts[0][g], parts[1][g])
        lg = _dot_nt(qa[g], kn_ref[0][:, g * DH_A:(g + 1) * DH_A]) * DSA_SCALE
        lg = lg.reshape(REP_A, SUBLANE, PAGE) + bn_ref[g]
        lg = jnp.where(sel_new[None], lg, -jnp.inf).reshape(rows, PAGE)
        _, l, acc = _online_update(carry, lg, vn_ref[0][:, g * DH_A:(g + 1) * DH_A])
        o_ref[0, g] = acc / l


def _sample_pass_b(page_table, scores, thr, qa_s, ka_new, va_new, bias_last, bias_new, far_s, cache_k, cache_v,
                   rs, pp):
    bs, n_pages = page_table.shape
    pack = SUBLANE // rs
    n_chunks = n_pages // pp
    ch = pp * PAGE
    blk = lambda *shape: pl.BlockSpec((1,) + shape, lambda s, pt: (s,) + (0,) * len(shape))
    pblk = lambda *shape: pl.BlockSpec((1,) + shape, lambda s, pt: (s // pack,) + (0,) * len(shape))
    cst = lambda shape: pl.BlockSpec(shape, lambda s, pt: (0,) * len(shape))
    any_spec = pl.BlockSpec(memory_space=pl.ANY)
    grid_spec = pltpu.PrefetchScalarGridSpec(
        num_scalar_prefetch=1,
        grid=(bs,),
        in_specs=[pblk(n_chunks + 1, SUBLANE, ch), pblk(SUBLANE, LANE), pblk(KVH_A, REP_A * SUBLANE, DH_A),
                  blk(PAGE, 256), blk(PAGE, 256),
                  cst((KVH_A, REP_A, SUBLANE, ch)), cst((KVH_A, REP_A, SUBLANE, PAGE)), cst((KVH_A, REP_A, 1, 1)),
                  any_spec, any_spec],
        out_specs=[blk(KVH_A, REP_A * SUBLANE, DH_A)],
        scratch_shapes=[pltpu.VMEM((2, KVH_A * ch, DH_A), f32), pltpu.VMEM((2, KVH_A * ch, DH_A), f32),
                        pltpu.SemaphoreType.DMA((2, 2))])
    return pl.pallas_call(
        functools.partial(_sample_b_kernel, n_pages=n_pages, pp=pp),
        grid_spec=grid_spec,
        out_shape=[jax.ShapeDtypeStruct((bs, KVH_A, REP_A * SUBLANE, DH_A), f32)],
        compiler_params=_cparams("arbitrary"),
        name="sample_dsa",
    )(page_table.reshape(-1), scores, thr, qa_s, ka_new, va_new, bias_last, bias_new, far_s, cache_k, cache_v)[0]


def _route_tile(lg):
    lane = lax.broadcasted_iota(i32, lg.shape, 1)
    lane_f = lane.astype(f32)
    first = lambda hit: jnp.min(jnp.where(hit, lane_f, float(LANE)), axis=1, keepdims=True)
    gl = jnp.where(lane < N_GROUPS, lg, -jnp.inf)
    gm = jnp.max(gl, axis=1, keepdims=True)
    g_top = 1.0 / jnp.sum(jnp.exp(gl - gm), axis=1, keepdims=True)
    lo = N_GROUPS + E_PER_GROUP * first(gl == gm)
    el = jnp.where((lane_f >= lo) & (lane_f < lo + E_PER_GROUP), lg, -jnp.inf)
    em1 = jnp.max(el, axis=1, keepdims=True)
    i1 = first(el == em1)
    el2 = jnp.where(lane_f == i1, -jnp.inf, el)
    em2 = jnp.max(el2, axis=1, keepdims=True)
    i2 = first(el2 == em2)
    r = jnp.exp(em2 - em1)
    w1 = g_top / (1.0 + r)
    w2 = g_top * r / (1.0 + r)
    out = jnp.where(lane == 0, i1 - N_GROUPS, 0.0)
    out = jnp.where(lane == 1, i2 - N_GROUPS, out)
    out = jnp.where(lane == 2, w1, out)
    return jnp.where(lane == 3, w2, out)


def _merge_kernel(oa_ref, ob_ref, ga_ref, gb_ref, h_ref, woa_ref, wuv_ref, wob_ref, wo_ref, gf_ref, wr_ref,
                  h1_ref, xn_ref, rt_ref):
    ya = _dot(oa_ref[...], woa_ref[...])
    ob = ob_ref[...]
    obv = jnp.concatenate(
        [_dot(ob[:, h * KV_LORA:(h + 1) * KV_LORA], wuv_ref[h]) for h in range(H_B)], axis=1)
    yb = _dot(obv.astype(bf16), wob_ref[...])
    z = jax.nn.sigmoid(ga_ref[...]) * ya + jax.nn.sigmoid(gb_ref[...]) * yb
    h1 = h_ref[...] + _dot(z.astype(bf16), wo_ref[...])
    h1_ref[...] = h1
    xn = _rms(h1, gf_ref[...])
    for j, blk in enumerate(_split_rows(xn)):
        xn_ref[:, j, :] = blk
    rt_ref[...] = _route_tile(_dot(xn.astype(bf16), wr_ref[...]))


def _merge(oa, ob, proj, h, woa, wuv, wob, wo, g_ffn, wr):
    n, d = h.shape
    tm = ROW_TILE
    row = lambda w: pl.BlockSpec((tm, w), lambda i: (i, 0))
    jga, jgb = _COL["ga"][0], _COL["gb"][0]
    return pl.pallas_call(
        _merge_kernel,
        grid=(n // tm,),
        in_specs=[row(oa.shape[1]), row(ob.shape[1]),
                  pl.BlockSpec((tm, d), lambda i: (i, jga)), pl.BlockSpec((tm, d), lambda i: (i, jgb)),
                  row(d), _const_spec(woa.shape), _const_spec(wuv.shape), _const_spec(wob.shape),
                  _const_spec(wo.shape), _const_spec((1, d)), _const_spec(wr.shape)],
        out_specs=[row(d), pl.BlockSpec((tm, d // LANE, LANE), lambda i: (i, 0, 0)), row(LANE)],
        out_shape=[jax.ShapeDtypeStruct((n, d), f32), jax.ShapeDtypeStruct((n, d // LANE, LANE), f32),
                   jax.ShapeDtypeStruct((n, LANE), f32)],
        compiler_params=_cparams("parallel"),
        name="merge_router",
    )(oa, ob, proj, proj, h, woa, wuv, wob, wo, g_ffn.reshape(1, d), wr)


def _gather_rows_start(idx_ref, base, n, src_hbm, dst, sem):
    def issue(r, c):
        pltpu.make_async_copy(src_hbm.at[idx_ref[base + r]], dst.at[:, r, :], sem).start()
        return c

    lax.fori_loop(0, n, issue, 0, unroll=8)


def _gather_rows_wait(n, src_hbm, dst, sem):
    def drain(r, c):
        pltpu.make_async_copy(src_hbm.at[0], dst.at[:, r, :], sem).wait()
        return c

    lax.fori_loop(0, n, drain, 0, unroll=8)


def _load_rows(buf):
    return jnp.concatenate([buf[j] for j in range(buf.shape[0])], axis=1)


def _moe_kernel(te_ref, src_ref, nu_ref, x_hbm, wt_ref, wg_ref, wu_ref, wd_ref, y_ref, xbuf, sems, *, tm):
    t = pl.program_id(0)
    nu = nu_ref[0]

    @pl.when(t == 0)
    def _():
        _gather_rows_start(src_ref, 0, tm, x_hbm, xbuf.at[0], sems.at[0])

    @pl.when(t + 1 < nu)
    def _():
        nxt = (t + 1) % 2
        _gather_rows_start(src_ref, (t + 1) * tm, tm, x_hbm, xbuf.at[nxt], sems.at[nxt])

    @pl.when(t < nu)
    def _():
        slot = t % 2
        _gather_rows_wait(tm, x_hbm, xbuf.at[slot], sems.at[slot])
        x = _load_rows(xbuf.at[slot]).astype(bf16)
        g = _dot(x, wg_ref[...].astype(bf16))
        u = _dot(x, wu_ref[...].astype(bf16))
        hh = (g * jax.nn.sigmoid(g)) * u * wt_ref[...]
        y = _dot(hh.astype(bf16), wd_ref[...].astype(bf16))
        for j, blk in enumerate(_split_rows(y)):
            y_ref[:, j, :] = blk

    @pl.when(t >= nu)
    def _():
        y_ref[...] = jnp.zeros_like(y_ref)


def _moe_experts(tile_expert, src_token, n_used, xn3, wt_sorted, w_gate, w_up, w_down, tm):
    p_total = src_token.shape[0]
    nl = xn3.shape[1]
    d = nl * LANE
    n_tiles = p_total // tm
    grid_spec = pltpu.PrefetchScalarGridSpec(
        num_scalar_prefetch=3,
        grid=(n_tiles,),
        in_specs=[pl.BlockSpec(memory_space=pl.ANY),
                  pl.BlockSpec((tm, 1), lambda t, te, src, nu: (t, 0)),
                  pl.BlockSpec((None, d, D_EXPERT), lambda t, te, src, nu: (te[t], 0, 0)),
                  pl.BlockSpec((None, d, D_EXPERT), lambda t, te, src, nu: (te[t], 0, 0)),
                  pl.BlockSpec((None, D_EXPERT, d), lambda t, te, src, nu: (te[t], 0, 0))],
        out_specs=pl.BlockSpec((tm, nl, LANE), lambda t, te, src, nu: (t, 0, 0)),
        scratch_shapes=[pltpu.VMEM((2, nl, tm, LANE), f32), pltpu.SemaphoreType.DMA((2,))])
    return pl.pallas_call(
        functools.partial(_moe_kernel, tm=tm),
        grid_spec=grid_spec,
        out_shape=jax.ShapeDtypeStruct((p_total, nl, LANE), f32),
        compiler_params=_cparams("arbitrary"),
        name="moe_experts",
    )(tile_expert, src_token, n_used, xn3, wt_sorted, w_gate, w_up, w_down)


def _combine_kernel(p0_ref, p1_ref, h_ref, g_ref, ys_hbm, o_ref, buf, sems, *, tm):
    i = pl.program_id(0)
    n_steps = pl.num_programs(0)

    def fetch(step, slot):
        _gather_rows_start(p0_ref, step * tm, tm, ys_hbm, buf.at[slot, 0], sems.at[slot, 0])
        _gather_rows_start(p1_ref, step * tm, tm, ys_hbm, buf.at[slot, 1], sems.at[slot, 1])

    @pl.when(i == 0)
    def _():
        fetch(0, 0)

    @pl.when(i + 1 < n_steps)
    def _():
        fetch(i + 1, (i + 1) % 2)

    slot = i % 2
    _gather_rows_wait(tm, ys_hbm, buf.at[slot, 0], sems.at[slot, 0])
    _gather_rows_wait(tm, ys_hbm, buf.at[slot, 1], sems.at[slot, 1])
    h2 = h_ref[...] + (_load_rows(buf.at[slot, 0]) + _load_rows(buf.at[slot, 1]))
    o_ref[...] = _rms(h2, g_ref[...])


def _combine(pos0, pos1, h1, g_final, y_sorted):
    n, d = h1.shape
    nl = d // LANE
    tm = ROW_TILE
    grid_spec = pltpu.PrefetchScalarGridSpec(
        num_scalar_prefetch=2,
        grid=(n // tm,),
        in_specs=[pl.BlockSpec((tm, d), lambda i, a, b: (i, 0)),
                  pl.BlockSpec((1, d), lambda i, a, b: (0, 0)),
                  pl.BlockSpec(memory_space=pl.ANY)],
        out_specs=pl.BlockSpec((tm, d), lambda i, a, b: (i, 0)),
        scratch_shapes=[pltpu.VMEM((2, 2, nl, tm, LANE), f32), pltpu.SemaphoreType.DMA((2, 2))])
    return pl.pallas_call(
        functools.partial(_combine_kernel, tm=tm),
        grid_spec=grid_spec,
        out_shape=jax.ShapeDtypeStruct((n, d), f32),
        compiler_params=_cparams("arbitrary"),
        name="moe_combine_norm",
    )(pos0, pos1, h1, g_final.reshape(1, d), y_sorted)


_IN_SIZES = (H_A * DH_A, KVH_A * DH_A, KVH_A * DH_A, H_IDX * D_IDX, H_IDX, D_IDX,
             Q_LORA, KV_LORA, D_ROPE, None, None)


def _pack_w_in(w_in, d):
    sizes = [d if s is None else s for s in _IN_SIZES]
    offs = np.cumsum([0] + sizes)
    qa, ka, va, qi, wi, ki, dq, dkv, krr, ga, gb = [w_in[:, offs[j]:offs[j + 1]] for j in range(11)]
    padw = lambda a, w: jnp.pad(a, ((0, 0), (0, w - a.shape[1])))
    half = D_ROPE // 2
    krs = jnp.concatenate([krr[:, half:], krr[:, :half]], axis=1)
    parts = dict(ga=ga, gb=gb, qa=qa, qi=qi, dq=dq, ka=ka, va=va, dkv=dkv, wi=padw(wi, LANE),
                 ki=padw(ki, LANE), krr=padw(krr, LANE), krs=padw(krs, LANE))
    cols = [parts[n] for n, _ in _SEGS]
    used = sum(w for _, w in _SEGS)
    cols.append(jnp.zeros((w_in.shape[0], PROJ_W - used), w_in.dtype))
    return jnp.concatenate(cols, axis=1).astype(bf16)


def _pack_w_uq(w_uq):
    w = w_uq.reshape(Q_LORA, H_B, D_NOPE + D_ROPE)
    nope = w[:, :, :D_NOPE].reshape(Q_LORA, H_B * D_NOPE)
    rope = w[:, :, D_NOPE:]
    half = D_ROPE // 2
    rope_sw = jnp.concatenate([rope[:, :, half:], rope[:, :, :half]], axis=2)
    padr = lambda a: jnp.pad(a, ((0, 0), (0, 0), (0, LANE - D_ROPE))).reshape(Q_LORA, H_B * LANE)
    return jnp.concatenate([nope, padr(rope), padr(rope_sw)], axis=1).astype(bf16)


def _rope_tables(pos):
    inv = ROPE_THETA ** (-jnp.arange(0, D_ROPE, 2, dtype=f32) / D_ROPE)
    ang = pos.astype(f32)[:, None] * inv[None, :]
    cos, sin = jnp.cos(ang), jnp.sin(ang)
    z = jnp.zeros((pos.shape[0], LANE - D_ROPE), f32)
    return jnp.concatenate([cos, cos, z], axis=1), jnp.concatenate([-sin, sin, z], axis=1)


def _rel_bucket(dist):
    n = jnp.maximum(dist, 0)
    max_exact = N_BUCKETS // 2
    nf = jnp.maximum(n, 1).astype(f32)
    large = max_exact + (jnp.log(nf / max_exact) / math.log(MAX_DIST / max_exact)
                         * (N_BUCKETS - max_exact)).astype(i32)
    large = jnp.minimum(large, N_BUCKETS - 1)
    return jnp.where(n < max_exact, n, large)


def _bias_of_dist(bias_tab, dist):
    return jnp.moveaxis(bias_tab[jnp.clip(dist, 0, MAX_DIST)], -1, 0)


def _sort_slots(eid, wts, tm):
    n = eid.shape[0]
    a = n * TOP_E
    e_flat = eid.reshape(a)
    onehot = (e_flat[:, None] == jnp.arange(N_EXPERTS, dtype=i32)[None, :]).astype(i32)
    rank = jnp.sum((jnp.cumsum(onehot, axis=0) - onehot) * onehot, axis=1)
    counts = jnp.sum(onehot, axis=0)
    padded = (counts + tm - 1) // tm * tm
    ends = jnp.cumsum(padded)
    starts = ends - padded
    pos = starts[e_flat] + rank
    p_total = (a + tm - 1) // tm * tm + N_EXPERTS * tm
    tok = (jnp.arange(a, dtype=i32) // TOP_E).astype(f32)
    slots = jnp.zeros((p_total, 2), f32).at[pos].set(jnp.stack([tok, wts.reshape(a)], axis=1))
    src_token = slots[:, 0].astype(i32)
    wt_sorted = slots[:, 1]
    tile_start = jnp.arange(p_total // tm, dtype=i32) * tm
    tile_expert = jnp.minimum(jnp.searchsorted(ends, tile_start, side="right"), N_EXPERTS - 1).astype(i32)
    n_used = (ends[-1] // tm).astype(i32).reshape(1)
    pos2 = pos.reshape(n, TOP_E)
    return tile_expert, src_token, n_used, wt_sorted.reshape(p_total, 1), pos2[:, 0], pos2[:, 1]


def _pages_per_chunk(n_pages):
    pp = 16
    while n_pages % pp:
        pp //= 2
    return pp


def kernel(x_prompt, x_sample, cache_k, cache_v, cache_idx_k, cache_ckv, cache_kr, page_table,
           meta_tokens, rel_bias, g_attn, w_in, g_q, w_uq, g_kv, w_uk, w_uv, w_oa, w_ob, w_o,
           g_ffn, w_rg, w_re, w_gate, w_up, w_down, g_final):
    nb, s_len, d = x_prompt.shape
    bs, ts, _ = x_sample.shape
    depth = w_in.shape[0]
    assert depth == 1 and ts <= SUBLANE
    rs = 4 if ts <= 4 else SUBLANE
    pack = SUBLANE // rs
    assert bs % pack == 0
    t_len = s_len + N_META
    tp = -(-t_len // QB) * QB
    n_pages = page_table.shape[1]
    past = n_pages * PAGE
    topk_p = min(TOPK_MAX, s_len // 4)
    topk_s = min(TOPK_MAX, (past + ts) // 4)
    n_prompt = nb * tp
    n_tok = n_prompt + bs * ts
    n_pad = -(-n_tok // ROW_TILE) * ROW_TILE
    tm_proj = 512 if n_pad % 512 == 0 else ROW_TILE

    meta = meta_tokens.astype(x_prompt.dtype)
    seq_pad = jnp.zeros((tp - t_len, d), x_prompt.dtype)
    pieces = []
    for b in range(nb):
        pieces += [meta, x_prompt[b], seq_pad]
    h0 = jnp.concatenate(pieces + [x_sample.reshape(bs * ts, d),
                                   jnp.zeros((n_pad - n_tok, d), x_prompt.dtype)], axis=0)
    pos = jnp.concatenate([jnp.tile(jnp.arange(tp, dtype=i32), nb),
                           jnp.tile(past + jnp.arange(ts, dtype=i32), bs),
                           jnp.zeros((n_pad - n_tok,), i32)])
    cos_t, sin_t = _rope_tables(pos)

    l = 0
    w_pack = _pack_w_in(w_in[l], d)
    wuq_p = _pack_w_uq(w_uq[l])
    wuk_t = jnp.transpose(w_uk[l], (1, 2, 0)).astype(bf16)
    wuv_p = jnp.transpose(w_uv[l], (1, 0, 2)).astype(bf16)
    wr = jnp.pad(jnp.concatenate([w_rg[l], w_re[l]], axis=1),
                 ((0, 0), (0, LANE - N_GROUPS - N_EXPERTS))).astype(bf16)

    proj = _norm_matmul(h0, g_attn[l], w_pack, tm_proj, 1024)
    qcat, ckv, kr, kcat, kab, vab, kib = _post_project(proj, cos_t, sin_t, g_q[l], g_kv[l], wuq_p, wuk_t)

    bias_tab = rel_bias[_rel_bucket(jnp.arange(MAX_DIST + 1, dtype=i32))].astype(f32)
    span = QB + BAND
    wvec = bias_tab[jnp.clip(jnp.arange(span, dtype=i32) - (QB - 1), 0, MAX_DIST)].T
    hank = jnp.tile(wvec, (1, QB + 1))[:, :QB * (span + 1)].reshape(H_A, QB, span + 1)
    band = hank[:, :, :BAND][:, :, ::-1]
    far1 = bias_tab[MAX_DIST]
    far = jnp.broadcast_to(far1[:, None, None], (H_A, 1, LANE))
    oa_p = _dsa_prompt(proj, kib, kab, vab, band, far, nb, tp, topk_p)
    oa_p = oa_p.transpose(0, 2, 1).reshape(n_prompt, H_A * DH_A)
    ob_p = _mla_prompt(qcat, kcat, nb, tp)

    pp = _pages_per_chunk(n_pages)
    ch = pp * PAGE
    nbp = bs // pack
    srow = lambda a: a[n_prompt:n_prompt + bs * ts]
    seg = lambda name: srow(proj)[:, _COL[name][0] * _COL[name][1]:(_COL[name][0] + 1) * _COL[name][1]]

    def pack_rows(a, lead):
        feat = a.shape[1:]
        a = a.reshape((nbp, pack, ts) + feat)
        a = jnp.pad(a, ((0, 0), (0, 0), (0, rs - ts)) + ((0, 0),) * len(feat))
        nl = len(lead)
        a = jnp.transpose(a, (0,) + tuple(range(3, 3 + nl)) + (1, 2, 3 + nl))
        return a.reshape((nbp,) + lead + (SUBLANE, feat[-1]))

    qi_s = pack_rows(seg("qi").reshape(bs * ts, H_IDX, D_IDX), (H_IDX,)).reshape(nbp, H_IDX * SUBLANE, D_IDX)
    wi_s = pack_rows(seg("wi")[:, :H_IDX].reshape(bs * ts, H_IDX, 1), (H_IDX,)).reshape(nbp, H_IDX * SUBLANE, 1)
    qa_s = pack_rows(seg("qa").reshape(bs * ts, KVH_A, REP_A, DH_A), (KVH_A, REP_A))
    qa_s = qa_s.reshape(nbp, KVH_A, REP_A * SUBLANE, DH_A)
    nblk_s = -(-(bs * ts) // QB)
    qc_s = qcat[n_prompt // QB:n_prompt // QB + nblk_s].transpose(0, 2, 1, 3)
    qc_s = qc_s.reshape(nblk_s * QB, H_B, MLA_KCAT)[:bs * ts].reshape(bs, ts, H_B, MLA_KCAT)
    qc_s = jnp.pad(qc_s, ((0, 0), (0, SUBLANE - ts), (0, 0), (0, 0))).transpose(0, 2, 1, 3)
    qc_s = qc_s.reshape(bs, H_B * SUBLANE, MLA_KCAT)
    padp = lambda a: jnp.pad(a.reshape(bs, ts, a.shape[-1]), ((0, 0), (0, PAGE - ts), (0, 0)))
    ki_new = padp(srow(kib)[:, :D_IDX])
    kc_new = padp(srow(kcat))
    ka_new = padp(srow(kab))
    va_new = padp(srow(vab))
    idx_t = jnp.swapaxes(cache_idx_k, 2, 3)
    kr_t = jnp.swapaxes(cache_kr, 2, 3)
    scores, ob_s = _sample_pass_a(page_table, qi_s, wi_s * IDX_W_SCALE, qc_s, ki_new, kc_new,
                                  idx_t, cache_ckv, kr_t, ts, rs, pp)
    thr = _sample_thresholds(scores, topk_s)
    j8 = (jnp.arange(SUBLANE, dtype=i32) % rs)[:, None]
    near = min(ch, BAND)
    bias_last = _bias_of_dist(bias_tab, j8 + near - jnp.arange(near, dtype=i32)[None, :])
    if ch > near:
        bias_last = jnp.concatenate(
            [jnp.broadcast_to(far1[:, None, None], (H_A, SUBLANE, ch - near)), bias_last], axis=2)
    bias_new = _bias_of_dist(bias_tab, j8 - jnp.arange(PAGE, dtype=i32)[None, :])
    shp = lambda a: a.reshape((KVH_A, REP_A) + a.shape[1:])
    oa_s = _sample_pass_b(page_table, scores, thr, qa_s, ka_new, va_new, shp(bias_last), shp(bias_new),
                          shp(far1[:, None, None]), cache_k.reshape(-1, PAGE * KVH_A, DH_A),
                          cache_v.reshape(-1, PAGE * KVH_A, DH_A), rs, pp)
    oa_s = oa_s.reshape(nbp, pack, KVH_A, REP_A, pack, rs, DH_A)
    oa_s = jnp.moveaxis(jnp.diagonal(oa_s, axis1=1, axis2=4), -1, 1)
    oa_s = oa_s[:, :, :, :, :ts].transpose(0, 1, 4, 2, 3, 5).reshape(bs * ts, H_A * DH_A).astype(bf16)
    ob_s = ob_s.reshape(bs, H_B, SUBLANE, KV_LORA)[:, :, :ts].transpose(0, 2, 1, 3)
    ob_s = ob_s.reshape(bs * ts, H_B * KV_LORA).astype(bf16)
    tail = lambda w: jnp.zeros((n_pad - n_tok, w), bf16)
    oa = jnp.concatenate([oa_p, oa_s, tail(H_A * DH_A)], axis=0)
    ob = jnp.concatenate([ob_p, ob_s, tail(H_B * KV_LORA)], axis=0)

    h1, xn3, route = _merge(oa, ob, proj, h0, w_oa[l].astype(bf16), wuv_p, w_ob[l].astype(bf16),
                            w_o[l].astype(bf16), g_ffn[l], wr)
    eid = route[:, 0:TOP_E].astype(i32)
    wts = route[:, TOP_E:2 * TOP_E]
    tile_expert, src_token, n_used, wt_sorted, pos0, pos1 = _sort_slots(eid, wts, ROW_TILE)
    y_sorted = _moe_experts(tile_expert, src_token, n_used, xn3, wt_sorted, w_gate[l], w_up[l], w_down[l],
                            ROW_TILE)
    y = _combine(pos0, pos1, h1, g_final, y_sorted)

    y_prompt = jnp.stack([y[b * tp + N_META:b * tp + t_len] for b in range(nb)])
    y_sample = y[n_prompt:n_tok].reshape(bs, ts, d)
    seg_all = lambda name: proj[:, _COL[name][0] * _COL[name][1]:(_COL[name][0] + 1) * _COL[name][1]]

    def states(rows, lead):
        ka = rows(seg_all("ka")).reshape(lead + (KVH_A, DH_A))
        va = rows(seg_all("va")).reshape(lead + (KVH_A, DH_A))
        ki = rows(seg_all("ki"))[..., :D_IDX].reshape(lead + (D_IDX,))
        cc = rows(ckv).reshape(lead + (KV_LORA,))
        rr = rows(kr)[..., :D_ROPE].reshape(lead + (D_ROPE,))
        return [a[None] for a in (ka, va, ki, cc, rr)]

    prow = lambda a: a[:n_prompt].reshape(nb, tp, a.shape[-1])[:, :t_len]
    st_p = states(prow, (nb, t_len))
    st_s = states(lambda a: a[n_prompt:n_tok], (bs, ts))
    return (y_prompt, y_sample, *st_p, *st_s)
```

```python
import functools
import math

import numpy as np
import jax
import jax.numpy as jnp
from jax import lax
from jax.experimental import pallas as pl
from jax.experimental.pallas import tpu as pltpu

N_META = 16
H_A, KVH_A, DH_A = 8, 2, 128
REP_A = H_A // KVH_A
H_IDX, D_IDX = 16, 64
TOPK_MAX = 256
IDX_W_SCALE = (H_IDX ** -0.5) * (D_IDX ** -0.5)
N_BUCKETS, MAX_DIST = 32, 128
H_B, Q_LORA, KV_LORA, D_NOPE, D_ROPE, D_V = 8, 512, 256, 128, 64, 128
ROPE_THETA = 10000.0
MLA_SCALE = (D_NOPE + D_ROPE) ** -0.5
DSA_SCALE = DH_A ** -0.5
N_GROUPS, E_PER_GROUP, TOP_E, D_EXPERT = 4, 8, 2, 512
N_EXPERTS = N_GROUPS * E_PER_GROUP
EPS = 1e-6
PAGE = 128

LANE = 128
SUBLANE = 8
VMEM_LIMIT_BYTES = 56 * 1024 * 1024

QB = 128
ROW_TILE = 256
MLA_KCAT = KV_LORA + LANE
BAND = 2 * QB
INT_MIN = -2 ** 31
NEG_INF_KEY = -2 ** 31 + 0x7FFFFF

_SEGS = (("ga", 2048), ("gb", 2048), ("qa", 1024), ("qi", 1024), ("dq", 512), ("ka", 256),
         ("va", 256), ("dkv", 256), ("wi", 128), ("ki", 128), ("krr", 128), ("krs", 128))
_COL = {}
_o = 0
for _n, _w in _SEGS:
    assert _o % _w == 0
    _COL[_n] = (_o // _w, _w)
    _o += _w
PROJ_W = 8192
assert _o <= PROJ_W

f32, bf16, i32 = jnp.float32, jnp.bfloat16, jnp.int32


def _cparams(*sem):
    return pltpu.CompilerParams(dimension_semantics=sem, vmem_limit_bytes=VMEM_LIMIT_BYTES)


def _dot(a, b):
    return jnp.dot(a, b, preferred_element_type=f32)


def _dot_nt(a, b):
    return lax.dot_general(a, b, (((1,), (1,)), ((), ())), preferred_element_type=f32)


def _rms(x, g):
    return x * lax.rsqrt(jnp.mean(x * x, axis=-1, keepdims=True) + EPS) * g


def _const_spec(shape):
    nd = len(shape)
    return pl.BlockSpec(shape, lambda *_: (0,) * nd, pipeline_mode=pl.Buffered(1))


def _split_rows(x):
    return [x[:, j * LANE:(j + 1) * LANE] for j in range(x.shape[1] // LANE)]


def _norm_matmul_kernel(x_ref, g_ref, w_ref, o_ref, xn_ref):
    @pl.when(pl.program_id(1) == 0)
    def _():
        xn_ref[...] = _rms(x_ref[...], g_ref[...]).astype(bf16)

    o_ref[...] = _dot(xn_ref[...], w_ref[...])


def _norm_matmul(x, g, w, tm, tn):
    n, k = x.shape
    m = w.shape[1]
    return pl.pallas_call(
        _norm_matmul_kernel,
        grid=(n // tm, m // tn),
        in_specs=[pl.BlockSpec((tm, k), lambda i, j: (i, 0)),
                  pl.BlockSpec((1, k), lambda i, j: (0, 0)),
                  pl.BlockSpec((k, tn), lambda i, j: (0, j))],
        out_specs=pl.BlockSpec((tm, tn), lambda i, j: (i, j)),
        out_shape=jax.ShapeDtypeStruct((n, m), f32),
        scratch_shapes=[pltpu.VMEM((tm, k), bf16)],
        compiler_params=_cparams("parallel", "arbitrary"),
        name="in_proj",
    )(x, g.reshape(1, k), w)


def _post_kernel(dq_ref, dkv_ref, krr_ref, krs_ref, ka_ref, va_ref, ki_ref, cos_ref, sin_ref,
                 gq_ref, gkv_ref, wuq_ref, wuk_ref,
                 qcat_ref, ckv_ref, kr_ref, kcat_ref, kab_ref, vab_ref, kib_ref):
    tm = dq_ref.shape[0]
    cos_t = cos_ref[...]
    sin_t = sin_ref[...]
    dqn = _rms(dq_ref[...], gq_ref[...]).astype(bf16)
    q = _dot(dqn, wuq_ref[...])
    for h in range(H_B):
        qlat = _dot(q[:, h * 128:(h + 1) * 128].astype(bf16), wuk_ref[h])
        qr = (q[:, 1024 + h * 128:1024 + (h + 1) * 128] * cos_t
              + q[:, 2048 + h * 128:2048 + (h + 1) * 128] * sin_t)
        for sb in range(tm // QB):
            qcat_ref[sb, h, :, 0:KV_LORA] = qlat[sb * QB:(sb + 1) * QB].astype(bf16)
            qcat_ref[sb, h, :, KV_LORA:MLA_KCAT] = qr[sb * QB:(sb + 1) * QB].astype(bf16)
    ckv = _rms(dkv_ref[...], gkv_ref[...])
    ckv_ref[...] = ckv
    kr = krr_ref[...] * cos_t + krs_ref[...] * sin_t
    kr_ref[...] = kr
    kcat_ref[:, 0:KV_LORA] = ckv.astype(bf16)
    kcat_ref[:, KV_LORA:MLA_KCAT] = kr.astype(bf16)
    kab_ref[...] = ka_ref[...].astype(bf16)
    vab_ref[...] = va_ref[...].astype(bf16)
    kib_ref[...] = ki_ref[...].astype(bf16)


def _post_project(proj, cos_t, sin_t, g_q, g_kv, wuq_p, wuk_t):
    n = proj.shape[0]
    tm = ROW_TILE

    def col(name):
        j, w = _COL[name]
        return pl.BlockSpec((tm, w), lambda i, j=j: (i, j))

    row = lambda w: pl.BlockSpec((tm, w), lambda i: (i, 0))
    outs = pl.pallas_call(
        _post_kernel,
        grid=(n // tm,),
        in_specs=[col("dq"), col("dkv"), col("krr"), col("krs"), col("ka"), col("va"), col("ki"),
                  row(LANE), row(LANE),
                  _const_spec((1, Q_LORA)), _const_spec((1, KV_LORA)),
                  _const_spec(wuq_p.shape), _const_spec(wuk_t.shape)],
        out_specs=[pl.BlockSpec((tm // QB, H_B, QB, MLA_KCAT), lambda i: (i, 0, 0, 0)),
                   row(KV_LORA), row(LANE), row(MLA_KCAT), row(256), row(256), row(LANE)],
        out_shape=[jax.ShapeDtypeStruct((n // QB, H_B, QB, MLA_KCAT), bf16),
                   jax.ShapeDtypeStruct((n, KV_LORA), f32),
                   jax.ShapeDtypeStruct((n, LANE), f32),
                   jax.ShapeDtypeStruct((n, MLA_KCAT), bf16),
                   jax.ShapeDtypeStruct((n, 256), bf16),
                   jax.ShapeDtypeStruct((n, 256), bf16),
                   jax.ShapeDtypeStruct((n, LANE), bf16)],
        compiler_params=_cparams("parallel"),
        name="post_proj",
    )(proj, proj, proj, proj, proj, proj, proj, cos_t, sin_t,
      g_q.reshape(1, -1), g_kv.reshape(1, -1), wuq_p, wuk_t)
    return outs


def _mla_prompt_kernel(q_ref, k_ref, o_ref, *, tk):
    i = pl.program_id(1)
    tp = k_ref.shape[0]
    t0 = i * QB
    rows = H_B * QB
    q = q_ref[0].reshape(rows, MLA_KCAT)
    qpos = t0 + (lax.broadcasted_iota(i32, (rows, 1), 0) & (QB - 1))
    nk = (t0 + QB + tk - 1) // tk

    def body(c, carry):
        m, l, acc = carry
        lo = c * tk
        st = pl.multiple_of(jnp.minimum(lo, tp - tk), QB)
        ks = k_ref[pl.ds(st, tk), :]
        s = _dot_nt(q, ks) * MLA_SCALE
        kpos = st + lax.broadcasted_iota(i32, (1, tk), 1)
        s = jnp.where((kpos <= qpos) & (kpos >= lo), s, -jnp.inf)
        m_new = jnp.maximum(m, jnp.max(s, axis=1, keepdims=True))
        p = jnp.exp(s - m_new)
        alpha = jnp.exp(m - m_new)
        l = alpha * l + jnp.sum(p, axis=1, keepdims=True)
        acc = alpha * acc + _dot(p.astype(bf16), ks[:, 0:KV_LORA])
        return m_new, l, acc

    m0 = jnp.full((rows, 1), -jnp.inf, f32)
    l0 = jnp.zeros((rows, 1), f32)
    a0 = jnp.zeros((rows, KV_LORA), f32)
    _, l, acc = lax.fori_loop(0, nk, body, (m0, l0, a0))
    o = acc / l
    for h in range(H_B):
        o_ref[:, h * KV_LORA:(h + 1) * KV_LORA] = o[h * QB:(h + 1) * QB].astype(bf16)


def _mla_prompt(qcat, kcat, nb, tp):
    nq = tp // QB
    tk = min(512, tp)
    return pl.pallas_call(
        functools.partial(_mla_prompt_kernel, tk=tk),
        grid=(nb, nq),
        in_specs=[pl.BlockSpec((1, H_B, QB, MLA_KCAT), lambda b, i: (b * nq + i, 0, 0, 0)),
                  pl.BlockSpec((tp, MLA_KCAT), lambda b, i: (b, 0))],
        out_specs=pl.BlockSpec((QB, H_B * KV_LORA), lambda b, i: (b * nq + i, 0)),
        out_shape=jax.ShapeDtypeStruct((nb * tp, H_B * KV_LORA), bf16),
        compiler_params=_cparams("parallel", "arbitrary"),
        name="mla_prompt",
    )(qcat, kcat)


def _sortable(score):
    b = lax.bitcast_convert_type(score, i32)
    return jnp.where(b < 0, b ^ jnp.int32(0x7FFFFFFF), b)


def _kth_largest(key_ref, k, count, shape1, two_bits=False):
    def n_ge(cand_u):
        return count((key_ref[...] >= (cand_u ^ jnp.int32(INT_MIN))).astype(i32))

    def body(it, res_u):
        cand_u = res_u | lax.shift_left(jnp.int32(1), 31 - it)
        return jnp.where(n_ge(cand_u) >= k, cand_u, res_u)

    def body2(it, res_u):
        c_hi = res_u | lax.shift_left(jnp.int32(1), 31 - 2 * it)
        c_lo = res_u | lax.shift_left(jnp.int32(1), 30 - 2 * it)
        c_both = c_hi | c_lo
        best = jnp.where(n_ge(c_lo) >= k, c_lo, res_u)
        best = jnp.where(n_ge(c_hi) >= k, c_hi, best)
        return jnp.where(n_ge(c_both) >= k, c_both, best)

    if two_bits:
        res_u = lax.fori_loop(0, 16, body2, jnp.zeros(shape1, i32), unroll=2)
    else:
        res_u = lax.fori_loop(0, 32, body, jnp.zeros(shape1, i32), unroll=4)
    return res_u ^ jnp.int32(INT_MIN)


def _online_update(carry, s, v):
    m, l, acc = carry
    m_new = jnp.maximum(m, jnp.max(s, axis=1, keepdims=True))
    m_safe = jnp.where(m_new == -jnp.inf, 0.0, m_new)
    p = jnp.exp(s - m_safe)
    alpha = jnp.exp(m - m_safe)
    l = alpha * l + jnp.sum(p, axis=1, keepdims=True)
    acc = alpha * acc + _dot(p.astype(bf16), v)
    return m_new, l, acc


def _online_update_t(carry, s, vt):
    m, l, acc = carry
    m_new = jnp.maximum(m, jnp.max(s, axis=0, keepdims=True))
    m_safe = jnp.where(m_new == -jnp.inf, 0.0, m_new)
    p = jnp.exp(s - m_safe)
    alpha = jnp.exp(m - m_safe)
    l = alpha * l + jnp.sum(p, axis=0, keepdims=True)
    acc = alpha * acc + _dot(vt, p.astype(bf16))
    return m_new, l, acc


def _dsa_prompt_kernel(qa_ref, qi_ref, wi_ref, ki_ref, ka_ref, vt_ref, band_ref, far_ref, o_ref,
                       key_ref, qs_ref, qg_ref, thr_ref, *, topk):
    i = pl.program_id(1)
    t0 = i * QB
    tp = ki_ref.shape[0]
    nck = tp // QB
    wide = 2 * QB
    qpos = t0 + lax.broadcasted_iota(i32, (1, QB), 1)

    qi = qi_ref[...]
    for h in range(H_IDX):
        qs_ref[h * QB:(h + 1) * QB, :] = qi[:, h * D_IDX:(h + 1) * D_IDX].astype(bf16)
    qa = qa_ref[...]
    for h in range(H_A):
        qg_ref[h * QB:(h + 1) * QB, :] = qa[:, h * DH_A:(h + 1) * DH_A].astype(bf16)
    wt = (wi_ref[...] * IDX_W_SCALE).T

    def score_chunk(c, carry):
        st = pl.multiple_of(jnp.minimum(c * wide, tp - wide), QB)
        s_all = _dot_nt(ki_ref[pl.ds(st, wide), 0:D_IDX], qs_ref[...])
        score = jnp.zeros((wide, QB), f32)
        for h in range(H_IDX):
            score = score + wt[h:h + 1, :] * jnp.maximum(s_all[:, h * QB:(h + 1) * QB], 0.0)
        kpos = st + lax.broadcasted_iota(i32, (wide, 1), 0)
        key = jnp.where(kpos <= qpos, _sortable(score), jnp.int32(NEG_INF_KEY))
        cj = st // QB
        key_ref[cj] = key[0:QB]
        key_ref[cj + 1] = key[QB:wide]
        return carry

    lax.fori_loop(0, (i + 2) // 2, score_chunk, 0)

    def blank_chunk(c, carry):
        key_ref[c] = jnp.full((QB, QB), NEG_INF_KEY, i32)
        return carry

    lax.fori_loop(i + 1, nck, blank_chunk, 0)

    def count_keys(ge):
        per_row = jnp.sum(ge, axis=0)
        per_group = jnp.sum(per_row.reshape(QB // SUBLANE, SUBLANE, QB), axis=0)
        return jnp.sum(per_group, axis=0, keepdims=True)[None]

    lo = 0
    for hi in sorted({-(-nck * part // 3) for part in (1, 2, 3)}):
        @pl.when((i >= lo) & (i < hi))
        def _():
            thr_ref[...] = _kth_largest(key_ref.at[0:hi], topk, count_keys, (1, 1, QB), two_bits=True)

        lo = hi
    thr = thr_ref[0]

    gcols = REP_A * QB

    def attend(state, st, cj, nk, sel, bias_of):
        vt = vt_ref[pl.ds(cj, nk)]
        vt = vt[0] if nk == 1 else jnp.concatenate([vt[n] for n in range(nk)], axis=1)
        new = []
        for g in range(KVH_A):
            cs = slice(g * gcols, (g + 1) * gcols)
            lg = _dot_nt(ka_ref[pl.ds(st, nk * QB), g * DH_A:(g + 1) * DH_A], qg_ref[cs, :]) * DSA_SCALE
            x = jnp.concatenate(
                [jnp.where(sel, lg[:, r * QB:(r + 1) * QB] + bias_of(REP_A * g + r), -jnp.inf)
                 for r in range(REP_A)], axis=1)
            new.append(_online_update_t(state[g], x, vt[g * DH_A:(g + 1) * DH_A]))
        return tuple(new)

    def far_chunk(c, state):
        st = pl.multiple_of(c * wide, wide)
        kk = key_ref[pl.ds(2 * c, 2)]
        key = jnp.concatenate([kk[0], kk[1]], axis=0)
        kpos = st + lax.broadcasted_iota(i32, (wide, 1), 0)
        return attend(state, st, 2 * c, 2, (key >= thr) & (kpos < t0 - QB), lambda h: far_ref[h][:, 0:1])

    init = (jnp.full((1, gcols), -jnp.inf, f32), jnp.zeros((1, gcols), f32), jnp.zeros((DH_A, gcols), f32))
    state = lax.fori_loop(0, i // 2, far_chunk, (init,) * KVH_A)
    ca = jnp.maximum(i - 1, 0)
    st_a = pl.multiple_of(ca * QB, QB)
    krow = lax.broadcasted_iota(i32, (QB, 1), 0)
    state = attend(state, st_a, ca, 1, (key_ref[ca] >= thr) & (st_a + krow < t0),
                   lambda h: band_ref[h][0:QB])
    state = attend(state, pl.multiple_of(t0, QB), i, 1, (key_ref[i] >= thr) & (t0 + krow <= qpos),
                   lambda h: band_ref[h][QB:BAND])
    for g in range(KVH_A):
        _, l, acc = state[g]
        o = acc / l
        for r in range(REP_A):
            h = REP_A * g + r
            o_ref[h * DH_A:(h + 1) * DH_A, :] = o[:, r * QB:(r + 1) * QB].astype(bf16)


def _dsa_prompt(proj, kib, kab, vab, band, far, nb, tp, topk):
    assert tp >= 2 * QB
    nq = tp // QB
    jqa, jqi, jwi = _COL["qa"][0], _COL["qi"][0], _COL["wi"][0]
    vt = vab[:nb * tp].reshape(nb, nq, QB, KVH_A * DH_A).transpose(0, 1, 3, 2)
    band_t = band.transpose(0, 2, 1)
    return pl.pallas_call(
        functools.partial(_dsa_prompt_kernel, topk=topk),
        grid=(nb, nq),
        in_specs=[pl.BlockSpec((QB, 1024), lambda b, i: (b * nq + i, jqa)),
                  pl.BlockSpec((QB, 1024), lambda b, i: (b * nq + i, jqi)),
                  pl.BlockSpec((QB, LANE), lambda b, i: (b * nq + i, jwi)),
                  pl.BlockSpec((tp, LANE), lambda b, i: (b, 0)),
                  pl.BlockSpec((tp, 256), lambda b, i: (b, 0)),
                  pl.BlockSpec((None, nq, KVH_A * DH_A, QB), lambda b, i: (b, 0, 0, 0)),
                  _const_spec((H_A, BAND, QB)),
                  _const_spec((H_A, 1, LANE))],
        out_specs=pl.BlockSpec((None, H_A * DH_A, QB), lambda b, i: (b, 0, i)),
        out_shape=jax.ShapeDtypeStruct((nb, H_A * DH_A, tp), bf16),
        scratch_shapes=[pltpu.VMEM((tp // QB, QB, QB), i32),
                        pltpu.VMEM((H_IDX * QB, D_IDX), bf16),
                        pltpu.VMEM((H_A * QB, DH_A), bf16),
                        pltpu.VMEM((1, 1, QB), i32)],
        compiler_params=_cparams("parallel", "arbitrary"),
        name="dsa_prompt",
    )(proj, proj, proj, kib, kab, vt, band_t, far)


def _page_copies(pt_ref, base, c, slot, pp, srcs, bufs, sems):
    out = []
    for p in range(pp):
        page = pt_ref[base + c * pp + p]
        for a, (src, buf) in enumerate(zip(srcs, bufs)):
            out.append(pltpu.make_async_copy(src(page), buf(slot, p), sems.at[slot, a]))
    return out


def _page_stream(pt_ref, n_pages, pp, srcs, bufs, sems):
    s_id = pl.program_id(0)
    n_seq = pl.num_programs(0)
    n_chunks = n_pages // pp

    def copies(seq, c, slot):
        return _page_copies(pt_ref, seq * n_pages, c, slot, pp, srcs, bufs, sems)

    def prologue():
        @pl.when(s_id == 0)
        def _():
            for cp in copies(0, 0, 0):
                cp.start()

    def advance(c):
        slot = (s_id * n_chunks + c) % 2

        @pl.when(c + 1 < n_chunks)
        def _():
            for cp in copies(s_id, c + 1, 1 - slot):
                cp.start()

        @pl.when((c + 1 == n_chunks) & (s_id + 1 < n_seq))
        def _():
            for cp in copies(s_id + 1, 0, 1 - slot):
                cp.start()

        for cp in copies(s_id, c, slot):
            cp.wait()
        return slot

    return prologue, advance


def _merge_softmax(a, b):
    (ma, la, acca), (mb, lb, accb) = a, b
    m = jnp.maximum(ma, mb)
    m_safe = jnp.where(m == -jnp.inf, 0.0, m)
    ea = jnp.exp(ma - m_safe)
    eb = jnp.exp(mb - m_safe)
    return m, la * ea + lb * eb, acca * ea + accb * eb


def _sample_a_kernel(pt_ref, qi_ref, wi_ref, qc_ref, kin_ref, kcn_ref, idx_hbm, ckv_hbm, kr_hbm,
                     sc_ref, ob_ref, ibuf, cbuf, rbuf, sems, *, n_pages, pp, ts, rs):
    s_id = pl.program_id(0)
    pack = SUBLANE // rs
    par = s_id % pack
    n_chunks = n_pages // pp
    ch = pp * PAGE
    srcs = (lambda pg: idx_hbm.at[0, pg], lambda pg: ckv_hbm.at[0, pg], lambda pg: kr_hbm.at[0, pg])
    bufs = (lambda sl, p: ibuf.at[sl, :, pl.ds(p * PAGE, PAGE)],
            lambda sl, p: cbuf.at[sl, pl.ds(p * PAGE, PAGE)],
            lambda sl, p: rbuf.at[sl, :, pl.ds(p * PAGE, PAGE)])
    prologue, advance = _page_stream(pt_ref, n_pages, pp, srcs, bufs, sems)

    qi = qi_ref[0].astype(bf16)
    w = wi_ref[0]
    qlat = qc_ref[0][:, 0:KV_LORA]
    qrope = qc_ref[0][:, KV_LORA:KV_LORA + D_ROPE]
    rows = H_B * SUBLANE

    def head_sum(s):
        s = jnp.maximum(s, 0.0) * w
        return jnp.sum(s.reshape(H_IDX, SUBLANE, s.shape[1]), axis=0)

    def put_scores(c, sc):
        for q in range(pack):
            @pl.when(par == q)
            def _():
                sc_ref[0, c, q * rs:(q + 1) * rs, :] = sc[q * rs:(q + 1) * rs]

    prologue()
    nsplit = 2 if ch >= 2 * LANE else 1
    part = ch // nsplit

    def body(c, carry):
        slot = advance(c)
        sc = head_sum(_dot(qi, ibuf[slot].astype(bf16)))
        new = []
        for k in range(nsplit):
            ck = cbuf[slot, k * part:(k + 1) * part].astype(bf16)
            rk = rbuf[slot, :, k * part:(k + 1) * part].astype(bf16)
            s = (_dot_nt(qlat, ck) + _dot(qrope, rk)) * MLA_SCALE
            new.append(_online_update(carry[k], s, ck))
        put_scores(c, sc)
        return tuple(new)

    init = (jnp.full((rows, 1), -jnp.inf, f32), jnp.zeros((rows, 1), f32),
            jnp.zeros((rows, KV_LORA), f32))
    parts = lax.fori_loop(0, n_chunks, body, (init,) * nsplit)
    carry = parts[0] if nsplit == 1 else _merge_softmax(parts[0], parts[1])

    kj = lax.broadcasted_iota(i32, (1, PAGE), 1)
    ok8 = (kj <= (lax.broadcasted_iota(i32, (SUBLANE, 1), 0) & (rs - 1))) & (kj < ts)
    sc_new = jnp.where(ok8, head_sum(_dot_nt(qi, kin_ref[0])), -jnp.inf)
    if ch > PAGE:
        sc_new = jnp.concatenate([sc_new, jnp.full((SUBLANE, ch - PAGE), -jnp.inf, f32)], axis=1)
    put_scores(n_chunks, sc_new)
    kcn = kcn_ref[0]
    okr = (kj <= (lax.broadcasted_iota(i32, (rows, 1), 0) & (SUBLANE - 1))) & (kj < ts)
    s = jnp.where(okr, _dot_nt(qc_ref[0], kcn) * MLA_SCALE, -jnp.inf)
    _, l, acc = _online_update(carry, s, kcn[:, 0:KV_LORA])
    ob_ref[0] = acc / l


def _sample_pass_a(page_table, qi_s, wi_s, qc_s, ki_new, kc_new, idx_t, cache_ckv, kr_t, ts, rs, pp):
    bs, n_pages = page_table.shape
    pack = SUBLANE // rs
    n_chunks = n_pages // pp
    ch = pp * PAGE
    blk = lambda *shape: pl.BlockSpec((1,) + shape, lambda s, pt: (s,) + (0,) * len(shape))
    pblk = lambda *shape: pl.BlockSpec((1,) + shape, lambda s, pt: (s // pack,) + (0,) * len(shape))
    any_spec = pl.BlockSpec(memory_space=pl.ANY)
    grid_spec = pltpu.PrefetchScalarGridSpec(
        num_scalar_prefetch=1,
        grid=(bs,),
        in_specs=[pblk(H_IDX * SUBLANE, D_IDX), pblk(H_IDX * SUBLANE, 1), blk(H_B * SUBLANE, MLA_KCAT),
                  blk(PAGE, D_IDX), blk(PAGE, MLA_KCAT), any_spec, any_spec, any_spec],
        out_specs=[pblk(n_chunks + 1, SUBLANE, ch), blk(H_B * SUBLANE, KV_LORA)],
        scratch_shapes=[pltpu.VMEM((2, D_IDX, ch), f32), pltpu.VMEM((2, ch, KV_LORA), f32),
                        pltpu.VMEM((2, D_ROPE, ch), f32), pltpu.SemaphoreType.DMA((2, 3))])
    return pl.pallas_call(
        functools.partial(_sample_a_kernel, n_pages=n_pages, pp=pp, ts=ts, rs=rs),
        grid_spec=grid_spec,
        out_shape=[jax.ShapeDtypeStruct((bs // pack, n_chunks + 1, SUBLANE, ch), f32),
                   jax.ShapeDtypeStruct((bs, H_B * SUBLANE, KV_LORA), f32)],
        compiler_params=_cparams("arbitrary"),
        name="sample_idx_mla",
    )(page_table.reshape(-1), qi_s, wi_s, qc_s, ki_new, kc_new, idx_t, cache_ckv, kr_t)


def _thr_kernel(sc_ref, thr_ref, key_ref, *, topk):
    g = sc_ref.shape[0]
    key_ref[...] = _sortable(sc_ref[...])
    thr = _kth_largest(key_ref, topk, lambda ge: jnp.sum(ge, axis=(1, 3), keepdims=True), (g, 1, SUBLANE, 1))
    thr_ref[...] = jnp.broadcast_to(thr[:, 0], (g, SUBLANE, LANE))


def _sample_thresholds(scores, topk):
    nblk, nc1, _, ch = scores.shape
    g = 8
    while nblk % g:
        g //= 2
    return pl.pallas_call(
        functools.partial(_thr_kernel, topk=topk),
        grid=(nblk // g,),
        in_specs=[pl.BlockSpec((g, nc1, SUBLANE, ch), lambda i: (i, 0, 0, 0))],
        out_specs=pl.BlockSpec((g, SUBLANE, LANE), lambda i: (i, 0, 0)),
        out_shape=jax.ShapeDtypeStruct((nblk, SUBLANE, LANE), i32),
        scratch_shapes=[pltpu.VMEM((g, nc1, SUBLANE, ch), i32)],
        compiler_params=_cparams("parallel"),
        name="sample_topk_thr",
    )(scores)


def _sample_b_kernel(pt_ref, sc_ref, thr_ref, qa_ref, kn_ref, vn_ref, bl_ref, bn_ref, far_ref, k_hbm, v_hbm,
                     o_ref, kbuf, vbuf, sems, *, n_pages, pp):
    n_chunks = n_pages // pp
    ch = pp * PAGE
    prow = PAGE * KVH_A
    srcs = (lambda pg: k_hbm.at[pg], lambda pg: v_hbm.at[pg])
    bufs = (lambda sl, p: kbuf.at[sl, pl.ds(p * prow, prow)], lambda sl, p: vbuf.at[sl, pl.ds(p * prow, prow)])
    prologue, advance = _page_stream(pt_ref, n_pages, pp, srcs, bufs, sems)
    prologue()
    thr = thr_ref[0][:, 0:1]
    rows = REP_A * SUBLANE
    qa = [qa_ref[0, g].astype(bf16) for g in range(KVH_A)]
    nsplit = 2 if ch >= 2 * LANE else 1
    part = ch // nsplit

    def attend(carry, sc, kk, vv, bias_of):
        sel = _sortable(sc) >= thr
        n = sel.shape[1]
        new = []
        for g in range(KVH_A):
            lg = _dot_nt(qa[g], kk(g)) * DSA_SCALE
            lg = lg.reshape(REP_A, SUBLANE, n) + bias_of(g)
            lg = jnp.where(sel[None], lg, -jnp.inf).reshape(rows, n)
            new.append(_online_update(carry[g], lg, vv(g)))
        return tuple(new)

    def body(c, carry):
        slot = advance(c)
        is_last = jnp.broadcast_to(c, (REP_A, SUBLANE, part)) == n_chunks - 1
        new = []
        for k in range(nsplit):
            lo = k * part
            head_rows = lambda buf, g: buf[slot, pl.ds(KVH_A * lo + g, part, stride=KVH_A), :].astype(bf16)
            new.append(attend(
                carry[k], sc_ref[0, c][:, lo:lo + part],
                lambda g: head_rows(kbuf, g), lambda g: head_rows(vbuf, g),
                lambda g: jnp.where(is_last, bl_ref[g][:, :, lo:lo + part], far_ref[g])))
        return tuple(new)

    init = (jnp.full((rows, 1), -jnp.inf, f32), jnp.zeros((rows, 1), f32), jnp.zeros((rows, DH_A), f32))
    parts = lax.fori_loop(0, n_chunks, body, ((init,) * KVH_A,) * nsplit)
    sel_new = _sortable(sc_ref[0, n_chunks][:, 0:PAGE]) >= thr
    for g in range(KVH_A):
        carry = parts[0][g] if nsplit == 1 else _merge_softmax(parts[0][g], parts[1][g])
        lg = _dot_nt(qa[g], kn_ref[0][:, g * DH_A:(g + 1) * DH_A]) * DSA_SCALE
        lg = lg.reshape(REP_A, SUBLANE, PAGE) + bn_ref[g]
        lg = jnp.where(sel_new[None], lg, -jnp.inf).reshape(rows, PAGE)
        _, l, acc = _online_update(carry, lg, vn_ref[0][:, g * DH_A:(g + 1) * DH_A])
        o_ref[0, g] = acc / l


def _sample_pass_b(page_table, scores, thr, qa_s, ka_new, va_new, bias_last, bias_new, far_s, cache_k, cache_v,
                   rs, pp):
    bs, n_pages = page_table.shape
    pack = SUBLANE // rs
    n_chunks = n_pages // pp
    ch = pp * PAGE
    blk = lambda *shape: pl.BlockSpec((1,) + shape, lambda s, pt: (s,) + (0,) * len(shape))
    pblk = lambda *shape: pl.BlockSpec((1,) + shape, lambda s, pt: (s // pack,) + (0,) * len(shape))
    cst = lambda shape: pl.BlockSpec(shape, lambda s, pt: (0,) * len(shape))
    any_spec = pl.BlockSpec(memory_space=pl.ANY)
    grid_spec = pltpu.PrefetchScalarGridSpec(
        num_scalar_prefetch=1,
        grid=(bs,),
        in_specs=[pblk(n_chunks + 1, SUBLANE, ch), pblk(SUBLANE, LANE), pblk(KVH_A, REP_A * SUBLANE, DH_A),
                  blk(PAGE, 256), blk(PAGE, 256),
                  cst((KVH_A, REP_A, SUBLANE, ch)), cst((KVH_A, REP_A, SUBLANE, PAGE)), cst((KVH_A, REP_A, 1, 1)),
                  any_spec, any_spec],
        out_specs=[blk(KVH_A, REP_A * SUBLANE, DH_A)],
        scratch_shapes=[pltpu.VMEM((2, KVH_A * ch, DH_A), f32), pltpu.VMEM((2, KVH_A * ch, DH_A), f32),
                        pltpu.SemaphoreType.DMA((2, 2))])
    return pl.pallas_call(
        functools.partial(_sample_b_kernel, n_pages=n_pages, pp=pp),
        grid_spec=grid_spec,
        out_shape=[jax.ShapeDtypeStruct((bs, KVH_A, REP_A * SUBLANE, DH_A), f32)],
        compiler_params=_cparams("arbitrary"),
        name="sample_dsa",
    )(page_table.reshape(-1), scores, thr, qa_s, ka_new, va_new, bias_last, bias_new, far_s, cache_k, cache_v)[0]


def _route_tile(lg):
    lane = lax.broadcasted_iota(i32, lg.shape, 1)
    lane_f = lane.astype(f32)
    first = lambda hit: jnp.min(jnp.where(hit, lane_f, float(LANE)), axis=1, keepdims=True)
    gl = jnp.where(lane < N_GROUPS, lg, -jnp.inf)
    gm = jnp.max(gl, axis=1, keepdims=True)
    g_top = 1.0 / jnp.sum(jnp.exp(gl - gm), axis=1, keepdims=True)
    lo = N_GROUPS + E_PER_GROUP * first(gl == gm)
    el = jnp.where((lane_f >= lo) & (lane_f < lo + E_PER_GROUP), lg, -jnp.inf)
    em1 = jnp.max(el, axis=1, keepdims=True)
    i1 = first(el == em1)
    el2 = jnp.where(lane_f == i1, -jnp.inf, el)
    em2 = jnp.max(el2, axis=1, keepdims=True)
    i2 = first(el2 == em2)
    r = jnp.exp(em2 - em1)
    w1 = g_top / (1.0 + r)
    w2 = g_top * r / (1.0 + r)
    out = jnp.where(lane == 0, i1 - N_GROUPS, 0.0)
    out = jnp.where(lane == 1, i2 - N_GROUPS, out)
    out = jnp.where(lane == 2, w1, out)
    return jnp.where(lane == 3, w2, out)


def _merge_kernel(oa_ref, ob_ref, ga_ref, gb_ref, h_ref, woa_ref, wuv_ref, wob_ref, wo_ref, gf_ref, wr_ref,
                  h1_ref, xn_ref, rt_ref):
    ya = _dot(oa_ref[...], woa_ref[...])
    ob = ob_ref[...]
    obv = jnp.concatenate(
        [_dot(ob[:, h * KV_LORA:(h + 1) * KV_LORA], wuv_ref[h]) for h in range(H_B)], axis=1)
    yb = _dot(obv.astype(bf16), wob_ref[...])
    z = jax.nn.sigmoid(ga_ref[...]) * ya + jax.nn.sigmoid(gb_ref[...]) * yb
    h1 = h_ref[...] + _dot(z.astype(bf16), wo_ref[...])
    h1_ref[...] = h1
    xn = _rms(h1, gf_ref[...])
    for j, blk in enumerate(_split_rows(xn)):
        xn_ref[:, j, :] = blk
    rt_ref[...] = _route_tile(_dot(xn.astype(bf16), wr_ref[...]))


def _merge(oa, ob, proj, h, woa, wuv, wob, wo, g_ffn, wr):
    n, d = h.shape
    tm = ROW_TILE
    row = lambda w: pl.BlockSpec((tm, w), lambda i: (i, 0))
    jga, jgb = _COL["ga"][0], _COL["gb"][0]
    return pl.pallas_call(
        _merge_kernel,
        grid=(n // tm,),
        in_specs=[row(oa.shape[1]), row(ob.shape[1]),
                  pl.BlockSpec((tm, d), lambda i: (i, jga)), pl.BlockSpec((tm, d), lambda i: (i, jgb)),
                  row(d), _const_spec(woa.shape), _const_spec(wuv.shape), _const_spec(wob.shape),
                  _const_spec(wo.shape), _const_spec((1, d)), _const_spec(wr.shape)],
        out_specs=[row(d), pl.BlockSpec((tm, d // LANE, LANE), lambda i: (i, 0, 0)), row(LANE)],
        out_shape=[jax.ShapeDtypeStruct((n, d), f32), jax.ShapeDtypeStruct((n, d // LANE, LANE), f32),
                   jax.ShapeDtypeStruct((n, LANE), f32)],
        compiler_params=_cparams("parallel"),
        name="merge_router",
    )(oa, ob, proj, proj, h, woa, wuv, wob, wo, g_ffn.reshape(1, d), wr)


def _gather_rows_start(idx_ref, base, n, src_hbm, dst, sem):
    def issue(r, c):
        pltpu.make_async_copy(src_hbm.at[idx_ref[base + r]], dst.at[:, r, :], sem).start()
        return c

    lax.fori_loop(0, n, issue, 0, unroll=8)


def _gather_rows_wait(n, src_hbm, dst, sem):
    def drain(r, c):
        pltpu.make_async_copy(src_hbm.at[0], dst.at[:, r, :], sem).wait()
        return c

    lax.fori_loop(0, n, drain, 0, unroll=8)


def _load_rows(buf):
    return jnp.concatenate([buf[j] for j in range(buf.shape[0])], axis=1)


def _moe_kernel(te_ref, src_ref, nu_ref, x_hbm, wg_ref, wu_ref, wd_ref, y_ref, xbuf, sems, *, tm):
    t = pl.program_id(0)
    nu = nu_ref[0]

    @pl.when(t == 0)
    def _():
        _gather_rows_start(src_ref, 0, tm, x_hbm, xbuf.at[0], sems.at[0])

    @pl.when(t + 1 < nu)
    def _():
        nxt = (t + 1) % 2
        _gather_rows_start(src_ref, (t + 1) * tm, tm, x_hbm, xbuf.at[nxt], sems.at[nxt])

    @pl.when(t < nu)
    def _():
        slot = t % 2
        _gather_rows_wait(tm, x_hbm, xbuf.at[slot], sems.at[slot])
        x = _load_rows(xbuf.at[slot]).astype(bf16)
        g = _dot(x, wg_ref[...].astype(bf16))
        u = _dot(x, wu_ref[...].astype(bf16))
        hh = (g * jax.nn.sigmoid(g)) * u
        y = _dot(hh.astype(bf16), wd_ref[...].astype(bf16))
        for j, blk in enumerate(_split_rows(y)):
            y_ref[:, j, :] = blk

    @pl.when(t >= nu)
    def _():
        y_ref[...] = jnp.zeros_like(y_ref)


def _moe_experts(tile_expert, src_token, n_used, xn3, w_gate, w_up, w_down, tm):
    p_total = src_token.shape[0]
    nl = xn3.shape[1]
    d = nl * LANE
    n_tiles = p_total // tm
    grid_spec = pltpu.PrefetchScalarGridSpec(
        num_scalar_prefetch=3,
        grid=(n_tiles,),
        in_specs=[pl.BlockSpec(memory_space=pl.ANY),
                  pl.BlockSpec((None, d, D_EXPERT), lambda t, te, src, nu: (te[t], 0, 0)),
                  pl.BlockSpec((None, d, D_EXPERT), lambda t, te, src, nu: (te[t], 0, 0)),
                  pl.BlockSpec((None, D_EXPERT, d), lambda t, te, src, nu: (te[t], 0, 0))],
        out_specs=pl.BlockSpec((tm, nl, LANE), lambda t, te, src, nu: (t, 0, 0)),
        scratch_shapes=[pltpu.VMEM((2, nl, tm, LANE), f32), pltpu.SemaphoreType.DMA((2,))])
    return pl.pallas_call(
        functools.partial(_moe_kernel, tm=tm),
        grid_spec=grid_spec,
        out_shape=jax.ShapeDtypeStruct((p_total, nl, LANE), f32),
        compiler_params=_cparams("arbitrary"),
        name="moe_experts",
    )(tile_expert, src_token, n_used, xn3, w_gate, w_up, w_down)


def _combine_kernel(p0_ref, p1_ref, h_ref, rt_ref, g_ref, ys_hbm, o_ref, buf, sems, *, tm):
    i = pl.program_id(0)
    n_steps = pl.num_programs(0)

    def fetch(step, slot):
        _gather_rows_start(p0_ref, step * tm, tm, ys_hbm, buf.at[slot, 0], sems.at[slot, 0])
        _gather_rows_start(p1_ref, step * tm, tm, ys_hbm, buf.at[slot, 1], sems.at[slot, 1])

    @pl.when(i == 0)
    def _():
        fetch(0, 0)

    @pl.when(i + 1 < n_steps)
    def _():
        fetch(i + 1, (i + 1) % 2)

    slot = i % 2
    _gather_rows_wait(tm, ys_hbm, buf.at[slot, 0], sems.at[slot, 0])
    _gather_rows_wait(tm, ys_hbm, buf.at[slot, 1], sems.at[slot, 1])
    rt = rt_ref[...]
    moe = (rt[:, TOP_E:TOP_E + 1] * _load_rows(buf.at[slot, 0])
           + rt[:, TOP_E + 1:TOP_E + 2] * _load_rows(buf.at[slot, 1]))
    o_ref[...] = _rms(h_ref[...] + moe, g_ref[...])


def _combine(pos0, pos1, h1, route, g_final, y_sorted):
    n, d = h1.shape
    nl = d // LANE
    tm = ROW_TILE
    grid_spec = pltpu.PrefetchScalarGridSpec(
        num_scalar_prefetch=2,
        grid=(n // tm,),
        in_specs=[pl.BlockSpec((tm, d), lambda i, a, b: (i, 0)),
                  pl.BlockSpec((tm, LANE), lambda i, a, b: (i, 0)),
                  pl.BlockSpec((1, d), lambda i, a, b: (0, 0)),
                  pl.BlockSpec(memory_space=pl.ANY)],
        out_specs=pl.BlockSpec((tm, d), lambda i, a, b: (i, 0)),
        scratch_shapes=[pltpu.VMEM((2, 2, nl, tm, LANE), f32), pltpu.SemaphoreType.DMA((2, 2))])
    return pl.pallas_call(
        functools.partial(_combine_kernel, tm=tm),
        grid_spec=grid_spec,
        out_shape=jax.ShapeDtypeStruct((n, d), f32),
        compiler_params=_cparams("arbitrary"),
        name="moe_combine_norm",
    )(pos0, pos1, h1, route, g_final.reshape(1, d), y_sorted)


_IN_SIZES = (H_A * DH_A, KVH_A * DH_A, KVH_A * DH_A, H_IDX * D_IDX, H_IDX, D_IDX,
             Q_LORA, KV_LORA, D_ROPE, None, None)


def _pack_w_in(w_in, d):
    sizes = [d if s is None else s for s in _IN_SIZES]
    offs = np.cumsum([0] + sizes)
    qa, ka, va, qi, wi, ki, dq, dkv, krr, ga, gb = [w_in[:, offs[j]:offs[j + 1]] for j in range(11)]
    padw = lambda a, w: jnp.pad(a, ((0, 0), (0, w - a.shape[1])))
    half = D_ROPE // 2
    krs = jnp.concatenate([krr[:, half:], krr[:, :half]], axis=1)
    parts = dict(ga=ga, gb=gb, qa=qa, qi=qi, dq=dq, ka=ka, va=va, dkv=dkv, wi=padw(wi, LANE),
                 ki=padw(ki, LANE), krr=padw(krr, LANE), krs=padw(krs, LANE))
    cols = [parts[n] for n, _ in _SEGS]
    used = sum(w for _, w in _SEGS)
    cols.append(jnp.zeros((w_in.shape[0], PROJ_W - used), w_in.dtype))
    return jnp.concatenate(cols, axis=1).astype(bf16)


def _pack_w_uq(w_uq):
    w = w_uq.reshape(Q_LORA, H_B, D_NOPE + D_ROPE)
    nope = w[:, :, :D_NOPE].reshape(Q_LORA, H_B * D_NOPE)
    rope = w[:, :, D_NOPE:]
    half = D_ROPE // 2
    rope_sw = jnp.concatenate([rope[:, :, half:], rope[:, :, :half]], axis=2)
    padr = lambda a: jnp.pad(a, ((0, 0), (0, 0), (0, LANE - D_ROPE))).reshape(Q_LORA, H_B * LANE)
    return jnp.concatenate([nope, padr(rope), padr(rope_sw)], axis=1).astype(bf16)


def _rope_tables(pos):
    inv = ROPE_THETA ** (-jnp.arange(0, D_ROPE, 2, dtype=f32) / D_ROPE)
    ang = pos.astype(f32)[:, None] * inv[None, :]
    cos, sin = jnp.cos(ang), jnp.sin(ang)
    z = jnp.zeros((pos.shape[0], LANE - D_ROPE), f32)
    return jnp.concatenate([cos, cos, z], axis=1), jnp.concatenate([-sin, sin, z], axis=1)


def _rel_bucket(dist):
    n = jnp.maximum(dist, 0)
    max_exact = N_BUCKETS // 2
    nf = jnp.maximum(n, 1).astype(f32)
    large = max_exact + (jnp.log(nf / max_exact) / math.log(MAX_DIST / max_exact)
                         * (N_BUCKETS - max_exact)).astype(i32)
    large = jnp.minimum(large, N_BUCKETS - 1)
    return jnp.where(n < max_exact, n, large)


def _bias_of_dist(bias_tab, dist):
    return jnp.moveaxis(bias_tab[jnp.clip(dist, 0, MAX_DIST)], -1, 0)


def _sort_slots(eid, tm):
    n = eid.shape[0]
    a = n * TOP_E
    e_flat = eid.reshape(a)
    onehot = (e_flat[:, None] == jnp.arange(N_EXPERTS, dtype=i32)[None, :]).astype(i32)
    rank = jnp.sum((jnp.cumsum(onehot, axis=0) - onehot) * onehot, axis=1)
    counts = jnp.sum(onehot, axis=0)
    padded = (counts + tm - 1) // tm * tm
    ends = jnp.cumsum(padded)
    starts = ends - padded
    pos = starts[e_flat] + rank
    p_total = (a + tm - 1) // tm * tm + N_EXPERTS * tm
    src_token = jnp.zeros((p_total,), i32).at[pos].set(jnp.arange(a, dtype=i32) // TOP_E)
    tile_start = jnp.arange(p_total // tm, dtype=i32) * tm
    tile_expert = jnp.minimum(jnp.searchsorted(ends, tile_start, side="right"), N_EXPERTS - 1).astype(i32)
    n_used = (ends[-1] // tm).astype(i32).reshape(1)
    pos2 = pos.reshape(n, TOP_E)
    return tile_expert, src_token, n_used, pos2[:, 0], pos2[:, 1]


def _pages_per_chunk(n_pages):
    pp = 16
    while n_pages % pp:
        pp //= 2
    return pp


def kernel(x_prompt, x_sample, cache_k, cache_v, cache_idx_k, cache_ckv, cache_kr, page_table,
           meta_tokens, rel_bias, g_attn, w_in, g_q, w_uq, g_kv, w_uk, w_uv, w_oa, w_ob, w_o,
           g_ffn, w_rg, w_re, w_gate, w_up, w_down, g_final):
    nb, s_len, d = x_prompt.shape
    bs, ts, _ = x_sample.shape
    depth = w_in.shape[0]
    assert depth == 1 and ts <= SUBLANE
    rs = 4 if ts <= 4 else SUBLANE
    pack = SUBLANE // rs
    assert bs % pack == 0
    t_len = s_len + N_META
    tp = -(-t_len // QB) * QB
    n_pages = page_table.shape[1]
    past = n_pages * PAGE
    topk_p = min(TOPK_MAX, s_len // 4)
    topk_s = min(TOPK_MAX, (past + ts) // 4)
    n_prompt = nb * tp
    n_tok = n_prompt + bs * ts
    n_pad = -(-n_tok // ROW_TILE) * ROW_TILE
    tm_proj = next(t for t in (1024, 512, ROW_TILE) if n_pad % t == 0)

    meta = meta_tokens.astype(x_prompt.dtype)
    seq_pad = jnp.zeros((tp - t_len, d), x_prompt.dtype)
    pieces = []
    for b in range(nb):
        pieces += [meta, x_prompt[b], seq_pad]
    h0 = jnp.concatenate(pieces + [x_sample.reshape(bs * ts, d),
                                   jnp.zeros((n_pad - n_tok, d), x_prompt.dtype)], axis=0)
    pos = jnp.concatenate([jnp.tile(jnp.arange(tp, dtype=i32), nb),
                           jnp.tile(past + jnp.arange(ts, dtype=i32), bs),
                           jnp.zeros((n_pad - n_tok,), i32)])
    cos_t, sin_t = _rope_tables(pos)

    l = 0
    w_pack = _pack_w_in(w_in[l], d)
    wuq_p = _pack_w_uq(w_uq[l])
    wuk_t = jnp.transpose(w_uk[l], (1, 2, 0)).astype(bf16)
    wuv_p = jnp.transpose(w_uv[l], (1, 0, 2)).astype(bf16)
    wr = jnp.pad(jnp.concatenate([w_rg[l], w_re[l]], axis=1),
                 ((0, 0), (0, LANE - N_GROUPS - N_EXPERTS))).astype(bf16)

    proj = _norm_matmul(h0, g_attn[l], w_pack, tm_proj, 1024)
    qcat, ckv, kr, kcat, kab, vab, kib = _post_project(proj, cos_t, sin_t, g_q[l], g_kv[l], wuq_p, wuk_t)

    bias_tab = rel_bias[_rel_bucket(jnp.arange(MAX_DIST + 1, dtype=i32))].astype(f32)
    span = QB + BAND
    wvec = bias_tab[jnp.clip(jnp.arange(span, dtype=i32) - (QB - 1), 0, MAX_DIST)].T
    hank = jnp.tile(wvec, (1, QB + 1))[:, :QB * (span + 1)].reshape(H_A, QB, span + 1)
    band = hank[:, :, :BAND][:, :, ::-1]
    far1 = bias_tab[MAX_DIST]
    far = jnp.broadcast_to(far1[:, None, None], (H_A, 1, LANE))
    oa_p = _dsa_prompt(proj, kib, kab, vab, band, far, nb, tp, topk_p)
    oa_p = oa_p.transpose(0, 2, 1).reshape(n_prompt, H_A * DH_A)
    ob_p = _mla_prompt(qcat, kcat, nb, tp)

    pp = _pages_per_chunk(n_pages)
    ch = pp * PAGE
    nbp = bs // pack
    srow = lambda a: a[n_prompt:n_prompt + bs * ts]
    seg = lambda name: srow(proj)[:, _COL[name][0] * _COL[name][1]:(_COL[name][0] + 1) * _COL[name][1]]

    def pack_rows(a, lead):
        feat = a.shape[1:]
        a = a.reshape((nbp, pack, ts) + feat)
        a = jnp.pad(a, ((0, 0), (0, 0), (0, rs - ts)) + ((0, 0),) * len(feat))
        nl = len(lead)
        a = jnp.transpose(a, (0,) + tuple(range(3, 3 + nl)) + (1, 2, 3 + nl))
        return a.reshape((nbp,) + lead + (SUBLANE, feat[-1]))

    qi_s = pack_rows(seg("qi").reshape(bs * ts, H_IDX, D_IDX), (H_IDX,)).reshape(nbp, H_IDX * SUBLANE, D_IDX)
    wi_s = pack_rows(seg("wi")[:, :H_IDX].reshape(bs * ts, H_IDX, 1), (H_IDX,)).reshape(nbp, H_IDX * SUBLANE, 1)
    qa_s = pack_rows(seg("qa").reshape(bs * ts, KVH_A, REP_A, DH_A), (KVH_A, REP_A))
    qa_s = qa_s.reshape(nbp, KVH_A, REP_A * SUBLANE, DH_A)
    nblk_s = -(-(bs * ts) // QB)
    qc_s = qcat[n_prompt // QB:n_prompt // QB + nblk_s].transpose(0, 2, 1, 3)
    qc_s = qc_s.reshape(nblk_s * QB, H_B, MLA_KCAT)[:bs * ts].reshape(bs, ts, H_B, MLA_KCAT)
    qc_s = jnp.pad(qc_s, ((0, 0), (0, SUBLANE - ts), (0, 0), (0, 0))).transpose(0, 2, 1, 3)
    qc_s = qc_s.reshape(bs, H_B * SUBLANE, MLA_KCAT)
    padp = lambda a: jnp.pad(a.reshape(bs, ts, a.shape[-1]), ((0, 0), (0, PAGE - ts), (0, 0)))
    ki_new = padp(srow(kib)[:, :D_IDX])
    kc_new = padp(srow(kcat))
    ka_new = padp(srow(kab))
    va_new = padp(srow(vab))
    idx_t = jnp.swapaxes(cache_idx_k, 2, 3)
    kr_t = jnp.swapaxes(cache_kr, 2, 3)
    scores, ob_s = _sample_pass_a(page_table, qi_s, wi_s * IDX_W_SCALE, qc_s, ki_new, kc_new,
                                  idx_t, cache_ckv, kr_t, ts, rs, pp)
    thr = _sample_thresholds(scores, topk_s)
    j8 = (jnp.arange(SUBLANE, dtype=i32) % rs)[:, None]
    near = min(ch, BAND)
    bias_last = _bias_of_dist(bias_tab, j8 + near - jnp.arange(near, dtype=i32)[None, :])
    if ch > near:
        bias_last = jnp.concatenate(
            [jnp.broadcast_to(far1[:, None, None], (H_A, SUBLANE, ch - near)), bias_last], axis=2)
    bias_new = _bias_of_dist(bias_tab, j8 - jnp.arange(PAGE, dtype=i32)[None, :])
    shp = lambda a: a.reshape((KVH_A, REP_A) + a.shape[1:])
    oa_s = _sample_pass_b(page_table, scores, thr, qa_s, ka_new, va_new, shp(bias_last), shp(bias_new),
                          shp(far1[:, None, None]), cache_k.reshape(-1, PAGE * KVH_A, DH_A),
                          cache_v.reshape(-1, PAGE * KVH_A, DH_A), rs, pp)
    oa_s = oa_s.reshape(nbp, pack, KVH_A, REP_A, pack, rs, DH_A)
    oa_s = jnp.moveaxis(jnp.diagonal(oa_s, axis1=1, axis2=4), -1, 1)
    oa_s = oa_s[:, :, :, :, :ts].transpose(0, 1, 4, 2, 3, 5).reshape(bs * ts, H_A * DH_A).astype(bf16)
    ob_s = ob_s.reshape(bs, H_B, SUBLANE, KV_LORA)[:, :, :ts].transpose(0, 2, 1, 3)
    ob_s = ob_s.reshape(bs * ts, H_B * KV_LORA).astype(bf16)
    tail = lambda w: jnp.zeros((n_pad - n_tok, w), bf16)
    oa = jnp.concatenate([oa_p, oa_s, tail(H_A * DH_A)], axis=0)
    ob = jnp.concatenate([ob_p, ob_s, tail(H_B * KV_LORA)], axis=0)

    h1, xn3, route = _merge(oa, ob, proj, h0, w_oa[l].astype(bf16), wuv_p, w_ob[l].astype(bf16),
                            w_o[l].astype(bf16), g_ffn[l], wr)
    eid = route[:, 0:TOP_E].astype(i32)
    tile_expert, src_token, n_used, pos0, pos1 = _sort_slots(eid, ROW_TILE)
    y_sorted = _moe_experts(tile_expert, src_token, n_used, xn3, w_gate[l], w_up[l], w_down[l], ROW_TILE)
    y = _combine(pos0, pos1, h1, route, g_final, y_sorted)

    y_prompt = jnp.stack([y[b * tp + N_META:b * tp + t_len] for b in range(nb)])
    y_sample = y[n_prompt:n_tok].reshape(bs, ts, d)
    seg_all = lambda name: proj[:, _COL[name][0] * _COL[name][1]:(_COL[name][0] + 1) * _COL[name][1]]

    def states(rows, lead):
        ka = rows(seg_all("ka")).reshape(lead + (KVH_A, DH_A))
        va = rows(seg_all("va")).reshape(lead + (KVH_A, DH_A))
        ki = rows(seg_all("ki"))[..., :D_IDX].reshape(lead + (D_IDX,))
        cc = rows(ckv).reshape(lead + (KV_LORA,))
        rr = rows(kr)[..., :D_ROPE].reshape(lead + (D_ROPE,))
        return [a[None] for a in (ka, va, ki, cc, rr)]

    prow = lambda a: a[:n_prompt].reshape(nb, tp, a.shape[-1])[:, :t_len]
    st_p = states(prow, (nb, t_len))
    st_s = states(lambda a: a[n_prompt:n_tok], (bs, ts))
    return (y_prompt, y_sample, *st_p, *st_s)
```

```python
import functools
import math

import numpy as np
import jax
import jax.numpy as jnp
from jax import lax
from jax.experimental import pallas as pl
from jax.experimental.pallas import tpu as pltpu

N_META = 16
H_A, KVH_A, DH_A = 8, 2, 128
REP_A = H_A // KVH_A
H_IDX, D_IDX = 16, 64
TOPK_MAX = 256
IDX_W_SCALE = (H_IDX ** -0.5) * (D_IDX ** -0.5)
N_BUCKETS, MAX_DIST = 32, 128
H_B, Q_LORA, KV_LORA, D_NOPE, D_ROPE, D_V = 8, 512, 256, 128, 64, 128
ROPE_THETA = 10000.0
MLA_SCALE = (D_NOPE + D_ROPE) ** -0.5
DSA_SCALE = DH_A ** -0.5
N_GROUPS, E_PER_GROUP, TOP_E, D_EXPERT = 4, 8, 2, 512
N_EXPERTS = N_GROUPS * E_PER_GROUP
EPS = 1e-6
PAGE = 128

LANE = 128
SUBLANE = 8
VMEM_LIMIT_BYTES = 56 * 1024 * 1024

QB = 128
ROW_TILE = 256
MLA_KCAT = KV_LORA + LANE
BAND = 2 * QB
INT_MIN = -2 ** 31
NEG_INF_KEY = -2 ** 31 + 0x7FFFFF

_SEGS = (("ga", 2048), ("gb", 2048), ("qa", 1024), ("qi", 1024), ("dq", 512), ("ka", 256),
         ("va", 256), ("dkv", 256), ("wi", 128), ("ki", 128), ("krr", 128), ("krs", 128))
_COL = {}
_o = 0
for _n, _w in _SEGS:
    assert _o % _w == 0
    _COL[_n] = (_o // _w, _w)
    _o += _w
PROJ_W = 8192
assert _o <= PROJ_W

f32, bf16, i32 = jnp.float32, jnp.bfloat16, jnp.int32


def _cparams(*sem):
    return pltpu.CompilerParams(dimension_semantics=sem, vmem_limit_bytes=VMEM_LIMIT_BYTES)


def _dot(a, b):
    return jnp.dot(a, b, preferred_element_type=f32)


def _dot_nt(a, b):
    return lax.dot_general(a, b, (((1,), (1,)), ((), ())), preferred_element_type=f32)


def _rms(x, g):
    return x * lax.rsqrt(jnp.mean(x * x, axis=-1, keepdims=True) + EPS) * g


def _const_spec(shape):
    nd = len(shape)
    return pl.BlockSpec(shape, lambda *_: (0,) * nd, pipeline_mode=pl.Buffered(1))


def _split_rows(x):
    return [x[:, j * LANE:(j + 1) * LANE] for j in range(x.shape[1] // LANE)]


def _norm_matmul_kernel(x_ref, g_ref, w_ref, o_ref, xn_ref):
    @pl.when(pl.program_id(1) == 0)
    def _():
        xn_ref[...] = _rms(x_ref[...], g_ref[...]).astype(bf16)

    o_ref[...] = _dot(xn_ref[...], w_ref[...])


def _norm_matmul(x, g, w, tm, tn):
    n, k = x.shape
    m = w.shape[1]
    return pl.pallas_call(
        _norm_matmul_kernel,
        grid=(n // tm, m // tn),
        in_specs=[pl.BlockSpec((tm, k), lambda i, j: (i, 0)),
                  pl.BlockSpec((1, k), lambda i, j: (0, 0)),
                  pl.BlockSpec((k, tn), lambda i, j: (0, j))],
        out_specs=pl.BlockSpec((tm, tn), lambda i, j: (i, j)),
        out_shape=jax.ShapeDtypeStruct((n, m), f32),
        scratch_shapes=[pltpu.VMEM((tm, k), bf16)],
        compiler_params=_cparams("parallel", "arbitrary"),
        name="in_proj",
    )(x, g.reshape(1, k), w)


def _post_kernel(dq_ref, dkv_ref, krr_ref, krs_ref, ka_ref, va_ref, ki_ref, cos_ref, sin_ref,
                 gq_ref, gkv_ref, wuq_ref, wuk_ref,
                 qcat_ref, ckv_ref, kr_ref, kcat_ref, kab_ref, vab_ref, kib_ref):
    tm = dq_ref.shape[0]
    cos_t = cos_ref[...]
    sin_t = sin_ref[...]
    dqn = _rms(dq_ref[...], gq_ref[...]).astype(bf16)
    q = _dot(dqn, wuq_ref[...])
    for h in range(H_B):
        qlat = _dot(q[:, h * 128:(h + 1) * 128].astype(bf16), wuk_ref[h])
        qr = (q[:, 1024 + h * 128:1024 + (h + 1) * 128] * cos_t
              + q[:, 2048 + h * 128:2048 + (h + 1) * 128] * sin_t)
        for sb in range(tm // QB):
            qcat_ref[sb, h, :, 0:KV_LORA] = qlat[sb * QB:(sb + 1) * QB].astype(bf16)
            qcat_ref[sb, h, :, KV_LORA:MLA_KCAT] = qr[sb * QB:(sb + 1) * QB].astype(bf16)
    ckv = _rms(dkv_ref[...], gkv_ref[...])
    ckv_ref[...] = ckv
    kr = krr_ref[...] * cos_t + krs_ref[...] * sin_t
    kr_ref[...] = kr
    kcat_ref[:, 0:KV_LORA] = ckv.astype(bf16)
    kcat_ref[:, KV_LORA:MLA_KCAT] = kr.astype(bf16)
    kab_ref[...] = ka_ref[...].astype(bf16)
    vab_ref[...] = va_ref[...].astype(bf16)
    kib_ref[...] = ki_ref[...].astype(bf16)


def _post_project(proj, cos_t, sin_t, g_q, g_kv, wuq_p, wuk_t):
    n = proj.shape[0]
    tm = ROW_TILE

    def col(name):
        j, w = _COL[name]
        return pl.BlockSpec((tm, w), lambda i, j=j: (i, j))

    row = lambda w: pl.BlockSpec((tm, w), lambda i: (i, 0))
    outs = pl.pallas_call(
        _post_kernel,
        grid=(n // tm,),
        in_specs=[col("dq"), col("dkv"), col("krr"), col("krs"), col("ka"), col("va"), col("ki"),
                  row(LANE), row(LANE),
                  _const_spec((1, Q_LORA)), _const_spec((1, KV_LORA)),
                  _const_spec(wuq_p.shape), _const_spec(wuk_t.shape)],
        out_specs=[pl.BlockSpec((tm // QB, H_B, QB, MLA_KCAT), lambda i: (i, 0, 0, 0)),
                   row(KV_LORA), row(LANE), row(MLA_KCAT), row(256), row(256), row(LANE)],
        out_shape=[jax.ShapeDtypeStruct((n // QB, H_B, QB, MLA_KCAT), bf16),
                   jax.ShapeDtypeStruct((n, KV_LORA), f32),
                   jax.ShapeDtypeStruct((n, LANE), f32),
                   jax.ShapeDtypeStruct((n, MLA_KCAT), bf16),
                   jax.ShapeDtypeStruct((n, 256), bf16),
                   jax.ShapeDtypeStruct((n, 256), bf16),
                   jax.ShapeDtypeStruct((n, LANE), bf16)],
        compiler_params=_cparams("parallel"),
        name="post_proj",
    )(proj, proj, proj, proj, proj, proj, proj, cos_t, sin_t,
      g_q.reshape(1, -1), g_kv.reshape(1, -1), wuq_p, wuk_t)
    return outs


def _mla_prompt_kernel(q_ref, k_ref, o_ref, *, tk):
    i = pl.program_id(1)
    tp = k_ref.shape[0]
    t0 = i * QB
    rows = H_B * QB
    q = q_ref[0].reshape(rows, MLA_KCAT)
    qpos = t0 + (lax.broadcasted_iota(i32, (rows, 1), 0) & (QB - 1))
    nk = (t0 + QB + tk - 1) // tk

    def body(c, carry):
        m, l, acc = carry
        lo = c * tk
        st = pl.multiple_of(jnp.minimum(lo, tp - tk), QB)
        ks = k_ref[pl.ds(st, tk), :]
        s = _dot_nt(q, ks) * MLA_SCALE
        kpos = st + lax.broadcasted_iota(i32, (1, tk), 1)
        s = jnp.where((kpos <= qpos) & (kpos >= lo), s, -jnp.inf)
        m_new = jnp.maximum(m, jnp.max(s, axis=1, keepdims=True))
        p = jnp.exp(s - m_new)
        alpha = jnp.exp(m - m_new)
        l = alpha * l + jnp.sum(p, axis=1, keepdims=True)
        acc = alpha * acc + _dot(p.astype(bf16), ks[:, 0:KV_LORA])
        return m_new, l, acc

    m0 = jnp.full((rows, 1), -jnp.inf, f32)
    l0 = jnp.zeros((rows, 1), f32)
    a0 = jnp.zeros((rows, KV_LORA), f32)
    _, l, acc = lax.fori_loop(0, nk, body, (m0, l0, a0))
    o = acc / l
    for h in range(H_B):
        o_ref[:, h * KV_LORA:(h + 1) * KV_LORA] = o[h * QB:(h + 1) * QB].astype(bf16)


def _mla_prompt(qcat, kcat, nb, tp):
    nq = tp // QB
    tk = min(512, tp)
    return pl.pallas_call(
        functools.partial(_mla_prompt_kernel, tk=tk),
        grid=(nb, nq),
        in_specs=[pl.BlockSpec((1, H_B, QB, MLA_KCAT), lambda b, i: (b * nq + i, 0, 0, 0)),
                  pl.BlockSpec((tp, MLA_KCAT), lambda b, i: (b, 0))],
        out_specs=pl.BlockSpec((QB, H_B * KV_LORA), lambda b, i: (b * nq + i, 0)),
        out_shape=jax.ShapeDtypeStruct((nb * tp, H_B * KV_LORA), bf16),
        compiler_params=_cparams("parallel", "arbitrary"),
        name="mla_prompt",
    )(qcat, kcat)


def _sortable(score):
    b = lax.bitcast_convert_type(score, i32)
    return jnp.where(b < 0, b ^ jnp.int32(0x7FFFFFFF), b)


def _kth_largest(key_ref, k, count, shape1, two_bits=False):
    def n_ge(cand_u):
        return count((key_ref[...] >= (cand_u ^ jnp.int32(INT_MIN))).astype(i32))

    def body(it, res_u):
        cand_u = res_u | lax.shift_left(jnp.int32(1), 31 - it)
        return jnp.where(n_ge(cand_u) >= k, cand_u, res_u)

    def body2(it, res_u):
        c_hi = res_u | lax.shift_left(jnp.int32(1), 31 - 2 * it)
        c_lo = res_u | lax.shift_left(jnp.int32(1), 30 - 2 * it)
        c_both = c_hi | c_lo
        best = jnp.where(n_ge(c_lo) >= k, c_lo, res_u)
        best = jnp.where(n_ge(c_hi) >= k, c_hi, best)
        return jnp.where(n_ge(c_both) >= k, c_both, best)

    if two_bits:
        res_u = lax.fori_loop(0, 16, body2, jnp.zeros(shape1, i32), unroll=2)
    else:
        res_u = lax.fori_loop(0, 32, body, jnp.zeros(shape1, i32), unroll=4)
    return res_u ^ jnp.int32(INT_MIN)


def _online_update(carry, s, v):
    m, l, acc = carry
    m_new = jnp.maximum(m, jnp.max(s, axis=1, keepdims=True))
    m_safe = jnp.where(m_new == -jnp.inf, 0.0, m_new)
    p = jnp.exp(s - m_safe)
    alpha = jnp.exp(m - m_safe)
    l = alpha * l + jnp.sum(p, axis=1, keepdims=True)
    acc = alpha * acc + _dot(p.astype(bf16), v)
    return m_new, l, acc


def _online_update_t(carry, s, vt):
    m, l, acc = carry
    m_new = jnp.maximum(m, jnp.max(s, axis=0, keepdims=True))
    m_safe = jnp.where(m_new == -jnp.inf, 0.0, m_new)
    p = jnp.exp(s - m_safe)
    alpha = jnp.exp(m - m_safe)
    l = alpha * l + jnp.sum(p, axis=0, keepdims=True)
    acc = alpha * acc + _dot(vt, p.astype(bf16))
    return m_new, l, acc


def _dsa_prompt_kernel(qa_ref, qi_ref, wi_ref, ki_ref, ka_ref, vt_ref, band_ref, far_ref, o_ref,
                       key_ref, qs_ref, qg_ref, thr_ref, *, topk):
    i = pl.program_id(1)
    t0 = i * QB
    tp = ki_ref.shape[0]
    nck = tp // QB
    wide = 2 * QB
    qpos = t0 + lax.broadcasted_iota(i32, (1, QB), 1)

    qi = qi_ref[...]
    for h in range(H_IDX):
        qs_ref[h * QB:(h + 1) * QB, :] = qi[:, h * D_IDX:(h + 1) * D_IDX].astype(bf16)
    qa = qa_ref[...]
    for h in range(H_A):
        qg_ref[h * QB:(h + 1) * QB, :] = qa[:, h * DH_A:(h + 1) * DH_A].astype(bf16)
    wt = (wi_ref[...] * IDX_W_SCALE).T

    def score_chunk(c, carry):
        st = pl.multiple_of(jnp.minimum(c * wide, tp - wide), QB)
        s_all = _dot_nt(ki_ref[pl.ds(st, wide), 0:D_IDX], qs_ref[...])
        score = jnp.zeros((wide, QB), f32)
        for h in range(H_IDX):
            score = score + wt[h:h + 1, :] * jnp.maximum(s_all[:, h * QB:(h + 1) * QB], 0.0)
        kpos = st + lax.broadcasted_iota(i32, (wide, 1), 0)
        key = jnp.where(kpos <= qpos, _sortable(score), jnp.int32(NEG_INF_KEY))
        cj = st // QB
        key_ref[cj] = key[0:QB]
        key_ref[cj + 1] = key[QB:wide]
        return carry

    lax.fori_loop(0, (i + 2) // 2, score_chunk, 0)

    def blank_chunk(c, carry):
        key_ref[c] = jnp.full((QB, QB), NEG_INF_KEY, i32)
        return carry

    lax.fori_loop(i + 1, nck, blank_chunk, 0)

    def count_keys(ge):
        per_row = jnp.sum(ge, axis=0)
        per_group = jnp.sum(per_row.reshape(QB // SUBLANE, SUBLANE, QB), axis=0)
        return jnp.sum(per_group, axis=0, keepdims=True)[None]

    lo = 0
    for hi in sorted({-(-nck * part // 3) for part in (1, 2, 3)}):
        @pl.when((i >= lo) & (i < hi))
        def _():
            thr_ref[...] = _kth_largest(key_ref.at[0:hi], topk, count_keys, (1, 1, QB), two_bits=True)

        lo = hi
    thr = thr_ref[0]

    gcols = REP_A * QB

    def attend(state, st, cj, nk, sel, bias_of):
        vt = vt_ref[pl.ds(cj, nk)]
        vt = vt[0] if nk == 1 else jnp.concatenate([vt[n] for n in range(nk)], axis=1)
        new = []
        for g in range(KVH_A):
            cs = slice(g * gcols, (g + 1) * gcols)
            lg = _dot_nt(ka_ref[pl.ds(st, nk * QB), g * DH_A:(g + 1) * DH_A], qg_ref[cs, :]) * DSA_SCALE
            x = jnp.concatenate(
                [jnp.where(sel, lg[:, r * QB:(r + 1) * QB] + bias_of(REP_A * g + r), -jnp.inf)
                 for r in range(REP_A)], axis=1)
            new.append(_online_update_t(state[g], x, vt[g * DH_A:(g + 1) * DH_A]))
        return tuple(new)

    def far_chunk(c, state):
        st = pl.multiple_of(c * wide, wide)
        kk = key_ref[pl.ds(2 * c, 2)]
        key = jnp.concatenate([kk[0], kk[1]], axis=0)
        kpos = st + lax.broadcasted_iota(i32, (wide, 1), 0)
        return attend(state, st, 2 * c, 2, (key >= thr) & (kpos < t0 - QB), lambda h: far_ref[h][:, 0:1])

    init = (jnp.full((1, gcols), -jnp.inf, f32), jnp.zeros((1, gcols), f32), jnp.zeros((DH_A, gcols), f32))
    state = lax.fori_loop(0, i // 2, far_chunk, (init,) * KVH_A)
    ca = jnp.maximum(i - 1, 0)
    st_a = pl.multiple_of(ca * QB, QB)
    krow = lax.broadcasted_iota(i32, (QB, 1), 0)
    state = attend(state, st_a, ca, 1, (key_ref[ca] >= thr) & (st_a + krow < t0),
                   lambda h: band_ref[h][0:QB])
    state = attend(state, pl.multiple_of(t0, QB), i, 1, (key_ref[i] >= thr) & (t0 + krow <= qpos),
                   lambda h: band_ref[h][QB:BAND])
    for g in range(KVH_A):
        _, l, acc = state[g]
        o = acc / l
        for r in range(REP_A):
            h = REP_A * g + r
            o_ref[h * DH_A:(h + 1) * DH_A, :] = o[:, r * QB:(r + 1) * QB].astype(bf16)


def _dsa_prompt(proj, kib, kab, vab, band, far, nb, tp, topk):
    assert tp >= 2 * QB
    nq = tp // QB
    jqa, jqi, jwi = _COL["qa"][0], _COL["qi"][0], _COL["wi"][0]
    vt = vab[:nb * tp].reshape(nb, nq, QB, KVH_A * DH_A).transpose(0, 1, 3, 2)
    band_t = band.transpose(0, 2, 1)
    return pl.pallas_call(
        functools.partial(_dsa_prompt_kernel, topk=topk),
        grid=(nb, nq),
        in_specs=[pl.BlockSpec((QB, 1024), lambda b, i: (b * nq + i, jqa)),
                  pl.BlockSpec((QB, 1024), lambda b, i: (b * nq + i, jqi)),
                  pl.BlockSpec((QB, LANE), lambda b, i: (b * nq + i, jwi)),
                  pl.BlockSpec((tp, LANE), lambda b, i: (b, 0)),
                  pl.BlockSpec((tp, 256), lambda b, i: (b, 0)),
                  pl.BlockSpec((None, nq, KVH_A * DH_A, QB), lambda b, i: (b, 0, 0, 0)),
                  _const_spec((H_A, BAND, QB)),
                  _const_spec((H_A, 1, LANE))],
        out_specs=pl.BlockSpec((None, H_A * DH_A, QB), lambda b, i: (b, 0, i)),
        out_shape=jax.ShapeDtypeStruct((nb, H_A * DH_A, tp), bf16),
        scratch_shapes=[pltpu.VMEM((tp // QB, QB, QB), i32),
                        pltpu.VMEM((H_IDX * QB, D_IDX), bf16),
                        pltpu.VMEM((H_A * QB, DH_A), bf16),
                        pltpu.VMEM((1, 1, QB), i32)],
        compiler_params=_cparams("parallel", "arbitrary"),
        name="dsa_prompt",
    )(proj, proj, proj, kib, kab, vt, band_t, far)


def _page_copies(pt_ref, base, c, slot, pp, srcs, bufs, sems):
    out = []
    for p in range(pp):
        page = pt_ref[base + c * pp + p]
        for a, (src, buf) in enumerate(zip(srcs, bufs)):
            out.append(pltpu.make_async_copy(src(page), buf(slot, p), sems.at[slot, a]))
    return out


def _page_stream(pt_ref, n_pages, pp, srcs, bufs, sems):
    s_id = pl.program_id(0)
    n_seq = pl.num_programs(0)
    n_chunks = n_pages // pp

    def copies(seq, c, slot):
        return _page_copies(pt_ref, seq * n_pages, c, slot, pp, srcs, bufs, sems)

    def prologue():
        @pl.when(s_id == 0)
        def _():
            for cp in copies(0, 0, 0):
                cp.start()

    def advance(c):
        slot = (s_id * n_chunks + c) % 2

        @pl.when(c + 1 < n_chunks)
        def _():
            for cp in copies(s_id, c + 1, 1 - slot):
                cp.start()

        @pl.when((c + 1 == n_chunks) & (s_id + 1 < n_seq))
        def _():
            for cp in copies(s_id + 1, 0, 1 - slot):
                cp.start()

        for cp in copies(s_id, c, slot):
            cp.wait()
        return slot

    return prologue, advance


def _merge_softmax(a, b):
    (ma, la, acca), (mb, lb, accb) = a, b
    m = jnp.maximum(ma, mb)
    m_safe = jnp.where(m == -jnp.inf, 0.0, m)
    ea = jnp.exp(ma - m_safe)
    eb = jnp.exp(mb - m_safe)
    return m, la * ea + lb * eb, acca * ea + accb * eb


def _sample_a_kernel(pt_ref, qi_ref, wi_ref, qc_ref, kin_ref, kcn_ref, idx_hbm, ckv_hbm, kr_hbm,
                     sc_ref, ob_ref, ibuf, cbuf, rbuf, sems, *, n_pages, pp, ts, rs):
    s_id = pl.program_id(0)
    pack = SUBLANE // rs
    par = s_id % pack
    n_chunks = n_pages // pp
    ch = pp * PAGE
    srcs = (lambda pg: idx_hbm.at[0, pg], lambda pg: ckv_hbm.at[0, pg], lambda pg: kr_hbm.at[0, pg])
    bufs = (lambda sl, p: ibuf.at[sl, :, pl.ds(p * PAGE, PAGE)],
            lambda sl, p: cbuf.at[sl, pl.ds(p * PAGE, PAGE)],
            lambda sl, p: rbuf.at[sl, :, pl.ds(p * PAGE, PAGE)])
    prologue, advance = _page_stream(pt_ref, n_pages, pp, srcs, bufs, sems)

    qi = qi_ref[0].astype(bf16)
    w = wi_ref[0]
    qlat = qc_ref[0][:, 0:KV_LORA]
    qrope = qc_ref[0][:, KV_LORA:KV_LORA + D_ROPE]
    rows = H_B * SUBLANE

    def head_sum(s):
        s = jnp.maximum(s, 0.0) * w
        return jnp.sum(s.reshape(H_IDX, SUBLANE, s.shape[1]), axis=0)

    def put_scores(c, sc):
        for q in range(pack):
            @pl.when(par == q)
            def _():
                sc_ref[0, c, q * rs:(q + 1) * rs, :] = sc[q * rs:(q + 1) * rs]

    prologue()
    nsplit = 2 if ch >= 2 * LANE else 1
    part = ch // nsplit

    def body(c, carry):
        slot = advance(c)
        sc = head_sum(_dot(qi, ibuf[slot].astype(bf16)))
        new = []
        for k in range(nsplit):
            ck = cbuf[slot, k * part:(k + 1) * part].astype(bf16)
            rk = rbuf[slot, :, k * part:(k + 1) * part].astype(bf16)
            s = (_dot_nt(qlat, ck) + _dot(qrope, rk)) * MLA_SCALE
            new.append(_online_update(carry[k], s, ck))
        put_scores(c, sc)
        return tuple(new)

    init = (jnp.full((rows, 1), -jnp.inf, f32), jnp.zeros((rows, 1), f32),
            jnp.zeros((rows, KV_LORA), f32))
    parts = lax.fori_loop(0, n_chunks, body, (init,) * nsplit)
    carry = parts[0] if nsplit == 1 else _merge_softmax(parts[0], parts[1])

    kj = lax.broadcasted_iota(i32, (1, PAGE), 1)
    ok8 = (kj <= (lax.broadcasted_iota(i32, (SUBLANE, 1), 0) & (rs - 1))) & (kj < ts)
    sc_new = jnp.where(ok8, head_sum(_dot_nt(qi, kin_ref[0])), -jnp.inf)
    if ch > PAGE:
        sc_new = jnp.concatenate([sc_new, jnp.full((SUBLANE, ch - PAGE), -jnp.inf, f32)], axis=1)
    put_scores(n_chunks, sc_new)
    kcn = kcn_ref[0]
    okr = (kj <= (lax.broadcasted_iota(i32, (rows, 1), 0) & (SUBLANE - 1))) & (kj < ts)
    s = jnp.where(okr, _dot_nt(qc_ref[0], kcn) * MLA_SCALE, -jnp.inf)
    _, l, acc = _online_update(carry, s, kcn[:, 0:KV_LORA])
    ob_ref[0] = acc / l


def _sample_pass_a(page_table, qi_s, wi_s, qc_s, ki_new, kc_new, idx_t, cache_ckv, kr_t, ts, rs, pp):
    bs, n_pages = page_table.shape
    pack = SUBLANE // rs
    n_chunks = n_pages // pp
    ch = pp * PAGE
    blk = lambda *shape: pl.BlockSpec((1,) + shape, lambda s, pt: (s,) + (0,) * len(shape))
    pblk = lambda *shape: pl.BlockSpec((1,) + shape, lambda s, pt: (s // pack,) + (0,) * len(shape))
    any_spec = pl.BlockSpec(memory_space=pl.ANY)
    grid_spec = pltpu.PrefetchScalarGridSpec(
        num_scalar_prefetch=1,
        grid=(bs,),
        in_specs=[pblk(H_IDX * SUBLANE, D_IDX), pblk(H_IDX * SUBLANE, 1), blk(H_B * SUBLANE, MLA_KCAT),
                  blk(PAGE, D_IDX), blk(PAGE, MLA_KCAT), any_spec, any_spec, any_spec],
        out_specs=[pblk(n_chunks + 1, SUBLANE, ch), blk(H_B * SUBLANE, KV_LORA)],
        scratch_shapes=[pltpu.VMEM((2, D_IDX, ch), f32), pltpu.VMEM((2, ch, KV_LORA), f32),
                        pltpu.VMEM((2, D_ROPE, ch), f32), pltpu.SemaphoreType.DMA((2, 3))])
    return pl.pallas_call(
        functools.partial(_sample_a_kernel, n_pages=n_pages, pp=pp, ts=ts, rs=rs),
        grid_spec=grid_spec,
        out_shape=[jax.ShapeDtypeStruct((bs // pack, n_chunks + 1, SUBLANE, ch), f32),
                   jax.ShapeDtypeStruct((bs, H_B * SUBLANE, KV_LORA), f32)],
        compiler_params=_cparams("arbitrary"),
        name="sample_idx_mla",
    )(page_table.reshape(-1), qi_s, wi_s, qc_s, ki_new, kc_new, idx_t, cache_ckv, kr_t)


def _thr_kernel(sc_ref, thr_ref, key_ref, *, topk):
    g = sc_ref.shape[0]
    key_ref[...] = _sortable(sc_ref[...])
    thr = _kth_largest(key_ref, topk, lambda ge: jnp.sum(ge, axis=(1, 3), keepdims=True), (g, 1, SUBLANE, 1))
    thr_ref[...] = jnp.broadcast_to(thr[:, 0], (g, SUBLANE, LANE))


def _sample_thresholds(scores, topk):
    nblk, nc1, _, ch = scores.shape
    g = 8
    while nblk % g:
        g //= 2
    return pl.pallas_call(
        functools.partial(_thr_kernel, topk=topk),
        grid=(nblk // g,),
        in_specs=[pl.BlockSpec((g, nc1, SUBLANE, ch), lambda i: (i, 0, 0, 0))],
        out_specs=pl.BlockSpec((g, SUBLANE, LANE), lambda i: (i, 0, 0)),
        out_shape=jax.ShapeDtypeStruct((nblk, SUBLANE, LANE), i32),
        scratch_shapes=[pltpu.VMEM((g, nc1, SUBLANE, ch), i32)],
        compiler_params=_cparams("parallel"),
        name="sample_topk_thr",
    )(scores)


def _sample_b_kernel(pt_ref, sc_ref, thr_ref, qa_ref, kn_ref, vn_ref, bl_ref, bn_ref, far_ref, k_hbm, v_hbm,
                     o_ref, kbuf, vbuf, sems, *, n_pages, pp):
    n_chunks = n_pages // pp
    ch = pp * PAGE
    prow = PAGE * KVH_A
    srcs = (lambda pg: k_hbm.at[pg], lambda pg: v_hbm.at[pg])
    bufs = (lambda sl, p: kbuf.at[sl, pl.ds(p * prow, prow)], lambda sl, p: vbuf.at[sl, pl.ds(p * prow, prow)])
    prologue, advance = _page_stream(pt_ref, n_pages, pp, srcs, bufs, sems)
    prologue()
    thr = thr_ref[0][:, 0:1]
    rows = REP_A * SUBLANE
    qa = [qa_ref[0, g].astype(bf16) for g in range(KVH_A)]
    nsplit = 1
    part = ch // nsplit

    def attend(carry, sc, kk, vv, bias_of):
        sel = _sortable(sc) >= thr
        n = sel.shape[1]
        new = []
        for g in range(KVH_A):
            lg = _dot_nt(qa[g], kk(g)) * DSA_SCALE
            lg = lg.reshape(REP_A, SUBLANE, n) + bias_of(g)
            lg = jnp.where(sel[None], lg, -jnp.inf).reshape(rows, n)
            new.append(_online_update(carry[g], lg, vv(g)))
        return tuple(new)

    def body(c, carry):
        slot = advance(c)
        is_last = jnp.broadcast_to(c, (REP_A, SUBLANE, part)) == n_chunks - 1
        new = []
        for k in range(nsplit):
            lo = k * part
            head_rows = lambda buf, g: buf[slot, pl.ds(KVH_A * lo + g, part, stride=KVH_A), :].astype(bf16)
            new.append(attend(
                carry[k], sc_ref[0, c][:, lo:lo + part],
                lambda g: head_rows(kbuf, g), lambda g: head_rows(vbuf, g),
                lambda g: jnp.where(is_last, bl_ref[g][:, :, lo:lo + part], far_ref[g])))
        return tuple(new)

    init = (jnp.full((rows, 1), -jnp.inf, f32), jnp.zeros((rows, 1), f32), jnp.zeros((rows, DH_A), f32))
    parts = lax.fori_loop(0, n_chunks, body, ((init,) * KVH_A,) * nsplit)
    sel_new = _sortable(sc_ref[0, n_chunks][:, 0:PAGE]) >= thr
    for g in range(KVH_A):
        carry = parts[0][g] if nsplit == 1 else _merge_softmax(parts[0][g], parts[1][g])
        lg = _dot_nt(qa[g], kn_ref[0][:, g * DH_A:(g + 1) * DH_A]) * DSA_SCALE
        lg = lg.reshape(REP_A, SUBLANE, PAGE) + bn_ref[g]
        lg = jnp.where(sel_new[None], lg, -jnp.inf).reshape(rows, PAGE)
        _, l, acc = _online_update(carry, lg, vn_ref[0][:, g * DH_A:(g + 1) * DH_A])
        o_ref[0, g] = acc / l


def _sample_pass_b(page_table, scores, thr, qa_s, ka_new, va_new, bias_last, bias_new, far_s, cache_k, cache_v,
                   rs, pp):
    bs, n_pages = page_table.shape
    pack = SUBLANE // rs
    n_chunks = n_pages // pp
    ch = pp * PAGE
    blk = lambda *shape: pl.BlockSpec((1,) + shape, lambda s, pt: (s,) + (0,) * len(shape))
    pblk = lambda *shape: pl.BlockSpec((1,) + shape, lambda s, pt: (s // pack,) + (0,) * len(shape))
    cst = lambda shape: pl.BlockSpec(shape, lambda s, pt: (0,) * len(shape))
    any_spec = pl.BlockSpec(memory_space=pl.ANY)
    grid_spec = pltpu.PrefetchScalarGridSpec(
        num_scalar_prefetch=1,
        grid=(bs,),
        in_specs=[pblk(n_chunks + 1, SUBLANE, ch), pblk(SUBLANE, LANE), pblk(KVH_A, REP_A * SUBLANE, DH_A),
                  blk(PAGE, 256), blk(PAGE, 256),
                  cst((KVH_A, REP_A, SUBLANE, ch)), cst((KVH_A, REP_A, SUBLANE, PAGE)), cst((KVH_A, REP_A, 1, 1)),
                  any_spec, any_spec],
        out_specs=[blk(KVH_A, REP_A * SUBLANE, DH_A)],
        scratch_shapes=[pltpu.VMEM((2, KVH_A * ch, DH_A), f32), pltpu.VMEM((2, KVH_A * ch, DH_A), f32),
                        pltpu.SemaphoreType.DMA((2, 2))])
    return pl.pallas_call(
        functools.partial(_sample_b_kernel, n_pages=n_pages, pp=pp),
        grid_spec=grid_spec,
        out_shape=[jax.ShapeDtypeStruct((bs, KVH_A, REP_A * SUBLANE, DH_A), f32)],
        compiler_params=_cparams("arbitrary"),
        name="sample_dsa",
    )(page_table.reshape(-1), scores, thr, qa_s, ka_new, va_new, bias_last, bias_new, far_s, cache_k, cache_v)[0]


def _route_tile(lg):
    lane = lax.broadcasted_iota(i32, lg.shape, 1)
    lane_f = lane.astype(f32)
    first = lambda hit: jnp.min(jnp.where(hit, lane_f, float(LANE)), axis=1, keepdims=True)
    gl = jnp.where(lane < N_GROUPS, lg, -jnp.inf)
    gm = jnp.max(gl, axis=1, keepdims=True)
    g_top = 1.0 / jnp.sum(jnp.exp(gl - gm), axis=1, keepdims=True)
    lo = N_GROUPS + E_PER_GROUP * first(gl == gm)
    el = jnp.where((lane_f >= lo) & (lane_f < lo + E_PER_GROUP), lg, -jnp.inf)
    em1 = jnp.max(el, axis=1, keepdims=True)
    i1 = first(el == em1)
    el2 = jnp.where(lane_f == i1, -jnp.inf, el)
    em2 = jnp.max(el2, axis=1, keepdims=True)
    i2 = first(el2 == em2)
    r = jnp.exp(em2 - em1)
    w1 = g_top / (1.0 + r)
    w2 = g_top * r / (1.0 + r)
    out = jnp.where(lane == 0, i1 - N_GROUPS, 0.0)
    out = jnp.where(lane == 1, i2 - N_GROUPS, out)
    out = jnp.where(lane == 2, w1, out)
    return jnp.where(lane == 3, w2, out)


def _merge_kernel(oa_ref, ob_ref, ga_ref, gb_ref, h_ref, woa_ref, wuv_ref, wob_ref, wo_ref, gf_ref, wr_ref,
                  h1_ref, xn_ref, rt_ref):
    ya = _dot(oa_ref[...], woa_ref[...])
    ob = ob_ref[...]
    obv = jnp.concatenate(
        [_dot(ob[:, h * KV_LORA:(h + 1) * KV_LORA], wuv_ref[h]) for h in range(H_B)], axis=1)
    yb = _dot(obv.astype(bf16), wob_ref[...])
    z = jax.nn.sigmoid(ga_ref[...]) * ya + jax.nn.sigmoid(gb_ref[...]) * yb
    h1 = h_ref[...] + _dot(z.astype(bf16), wo_ref[...])
    h1_ref[...] = h1
    xn = _rms(h1, gf_ref[...])
    for j, blk in enumerate(_split_rows(xn)):
        xn_ref[:, j, :] = blk
    rt_ref[...] = _route_tile(_dot(xn.astype(bf16), wr_ref[...]))


def _merge(oa, ob, proj, h, woa, wuv, wob, wo, g_ffn, wr):
    n, d = h.shape
    tm = ROW_TILE
    row = lambda w: pl.BlockSpec((tm, w), lambda i: (i, 0))
    jga, jgb = _COL["ga"][0], _COL["gb"][0]
    return pl.pallas_call(
        _merge_kernel,
        grid=(n // tm,),
        in_specs=[row(oa.shape[1]), row(ob.shape[1]),
                  pl.BlockSpec((tm, d), lambda i: (i, jga)), pl.BlockSpec((tm, d), lambda i: (i, jgb)),
                  row(d), _const_spec(woa.shape), _const_spec(wuv.shape), _const_spec(wob.shape),
                  _const_spec(wo.shape), _const_spec((1, d)), _const_spec(wr.shape)],
        out_specs=[row(d), pl.BlockSpec((tm, d // LANE, LANE), lambda i: (i, 0, 0)), row(LANE)],
        out_shape=[jax.ShapeDtypeStruct((n, d), f32), jax.ShapeDtypeStruct((n, d // LANE, LANE), f32),
                   jax.ShapeDtypeStruct((n, LANE), f32)],
        compiler_params=_cparams("parallel"),
        name="merge_router",
    )(oa, ob, proj, proj, h, woa, wuv, wob, wo, g_ffn.reshape(1, d), wr)


def _gather_rows_start(idx_ref, base, n, src_hbm, dst, sem):
    def issue(r, c):
        pltpu.make_async_copy(src_hbm.at[idx_ref[base + r]], dst.at[:, r, :], sem).start()
        return c

    lax.fori_loop(0, n, issue, 0, unroll=8)


def _gather_rows_wait(n, src_hbm, dst, sem):
    def drain(r, c):
        pltpu.make_async_copy(src_hbm.at[0], dst.at[:, r, :], sem).wait()
        return c

    lax.fori_loop(0, n, drain, 0, unroll=8)


def _load_rows(buf):
    return jnp.concatenate([buf[j] for j in range(buf.shape[0])], axis=1)


def _moe_kernel(te_ref, src_ref, nu_ref, x_hbm, wg_ref, wu_ref, wd_ref, y_ref, xbuf, sems, *, tm):
    t = pl.program_id(0)
    nu = nu_ref[0]

    @pl.when(t == 0)
    def _():
        _gather_rows_start(src_ref, 0, tm, x_hbm, xbuf.at[0], sems.at[0])

    @pl.when(t + 1 < nu)
    def _():
        nxt = (t + 1) % 2
        _gather_rows_start(src_ref, (t + 1) * tm, tm, x_hbm, xbuf.at[nxt], sems.at[nxt])

    @pl.when(t < nu)
    def _():
        slot = t % 2
        _gather_rows_wait(tm, x_hbm, xbuf.at[slot], sems.at[slot])
        x = _load_rows(xbuf.at[slot]).astype(bf16)
        g = _dot(x, wg_ref[...].astype(bf16))
        u = _dot(x, wu_ref[...].astype(bf16))
        hh = (g * jax.nn.sigmoid(g)) * u
        y = _dot(hh.astype(bf16), wd_ref[...].astype(bf16))
        for j, blk in enumerate(_split_rows(y)):
            y_ref[:, j, :] = blk

    @pl.when(t >= nu)
    def _():
        y_ref[...] = jnp.zeros_like(y_ref)


def _moe_experts(tile_expert, src_token, n_used, xn3, w_gate, w_up, w_down, tm):
    p_total = src_token.shape[0]
    nl = xn3.shape[1]
    d = nl * LANE
    n_tiles = p_total // tm
    grid_spec = pltpu.PrefetchScalarGridSpec(
        num_scalar_prefetch=3,
        grid=(n_tiles,),
        in_specs=[pl.BlockSpec(memory_space=pl.ANY),
                  pl.BlockSpec((None, d, D_EXPERT), lambda t, te, src, nu: (te[t], 0, 0)),
                  pl.BlockSpec((None, d, D_EXPERT), lambda t, te, src, nu: (te[t], 0, 0)),
                  pl.BlockSpec((None, D_EXPERT, d), lambda t, te, src, nu: (te[t], 0, 0))],
        out_specs=pl.BlockSpec((tm, nl, LANE), lambda t, te, src, nu: (t, 0, 0)),
        scratch_shapes=[pltpu.VMEM((2, nl, tm, LANE), f32), pltpu.SemaphoreType.DMA((2,))])
    return pl.pallas_call(
        functools.partial(_moe_kernel, tm=tm),
        grid_spec=grid_spec,
        out_shape=jax.ShapeDtypeStruct((p_total, nl, LANE), f32),
        compiler_params=_cparams("arbitrary"),
        name="moe_experts",
    )(tile_expert, src_token, n_used, xn3, w_gate, w_up, w_down)


def _combine_kernel(p0_ref, p1_ref, h_ref, rt_ref, g_ref, ys_hbm, o_ref, buf, sems, *, tm):
    i = pl.program_id(0)
    n_steps = pl.num_programs(0)

    def fetch(step, slot):
        _gather_rows_start(p0_ref, step * tm, tm, ys_hbm, buf.at[slot, 0], sems.at[slot, 0])
        _gather_rows_start(p1_ref, step * tm, tm, ys_hbm, buf.at[slot, 1], sems.at[slot, 1])

    @pl.when(i == 0)
    def _():
        fetch(0, 0)

    @pl.when(i + 1 < n_steps)
    def _():
        fetch(i + 1, (i + 1) % 2)

    slot = i % 2
    _gather_rows_wait(tm, ys_hbm, buf.at[slot, 0], sems.at[slot, 0])
    _gather_rows_wait(tm, ys_hbm, buf.at[slot, 1], sems.at[slot, 1])
    rt = rt_ref[...]
    moe = (rt[:, TOP_E:TOP_E + 1] * _load_rows(buf.at[slot, 0])
           + rt[:, TOP_E + 1:TOP_E + 2] * _load_rows(buf.at[slot, 1]))
    o_ref[...] = _rms(h_ref[...] + moe, g_ref[...])


def _combine(pos0, pos1, h1, route, g_final, y_sorted):
    n, d = h1.shape
    nl = d // LANE
    tm = ROW_TILE
    grid_spec = pltpu.PrefetchScalarGridSpec(
        num_scalar_prefetch=2,
        grid=(n // tm,),
        in_specs=[pl.BlockSpec((tm, d), lambda i, a, b: (i, 0)),
                  pl.BlockSpec((tm, LANE), lambda i, a, b: (i, 0)),
                  pl.BlockSpec((1, d), lambda i, a, b: (0, 0)),
                  pl.BlockSpec(memory_space=pl.ANY)],
        out_specs=pl.BlockSpec((tm, d), lambda i, a, b: (i, 0)),
        scratch_shapes=[pltpu.VMEM((2, 2, nl, tm, LANE), f32), pltpu.SemaphoreType.DMA((2, 2))])
    return pl.pallas_call(
        functools.partial(_combine_kernel, tm=tm),
        grid_spec=grid_spec,
        out_shape=jax.ShapeDtypeStruct((n, d), f32),
        compiler_params=_cparams("arbitrary"),
        name="moe_combine_norm",
    )(pos0, pos1, h1, route, g_final.reshape(1, d), y_sorted)


_IN_SIZES = (H_A * DH_A, KVH_A * DH_A, KVH_A * DH_A, H_IDX * D_IDX, H_IDX, D_IDX,
             Q_LORA, KV_LORA, D_ROPE, None, None)


def _pack_w_in(w_in, d):
    sizes = [d if s is None else s for s in _IN_SIZES]
    offs = np.cumsum([0] + sizes)
    qa, ka, va, qi, wi, ki, dq, dkv, krr, ga, gb = [w_in[:, offs[j]:offs[j + 1]] for j in range(11)]
    padw = lambda a, w: jnp.pad(a, ((0, 0), (0, w - a.shape[1])))
    half = D_ROPE // 2
    krs = jnp.concatenate([krr[:, half:], krr[:, :half]], axis=1)
    parts = dict(ga=ga, gb=gb, qa=qa, qi=qi, dq=dq, ka=ka, va=va, dkv=dkv, wi=padw(wi, LANE),
                 ki=padw(ki, LANE), krr=padw(krr, LANE), krs=padw(krs, LANE))
    cols = [parts[n] for n, _ in _SEGS]
    used = sum(w for _, w in _SEGS)
    cols.append(jnp.zeros((w_in.shape[0], PROJ_W - used), w_in.dtype))
    return jnp.concatenate(cols, axis=1).astype(bf16)


def _pack_w_uq(w_uq):
    w = w_uq.reshape(Q_LORA, H_B, D_NOPE + D_ROPE)
    nope = w[:, :, :D_NOPE].reshape(Q_LORA, H_B * D_NOPE)
    rope = w[:, :, D_NOPE:]
    half = D_ROPE // 2
    rope_sw = jnp.concatenate([rope[:, :, half:], rope[:, :, :half]], axis=2)
    padr = lambda a: jnp.pad(a, ((0, 0), (0, 0), (0, LANE - D_ROPE))).reshape(Q_LORA, H_B * LANE)
    return jnp.concatenate([nope, padr(rope), padr(rope_sw)], axis=1).astype(bf16)


def _rope_tables(pos):
    inv = ROPE_THETA ** (-jnp.arange(0, D_ROPE, 2, dtype=f32) / D_ROPE)
    ang = pos.astype(f32)[:, None] * inv[None, :]
    cos, sin = jnp.cos(ang), jnp.sin(ang)
    z = jnp.zeros((pos.shape[0], LANE - D_ROPE), f32)
    return jnp.concatenate([cos, cos, z], axis=1), jnp.concatenate([-sin, sin, z], axis=1)


def _rel_bucket(dist):
    n = jnp.maximum(dist, 0)
    max_exact = N_BUCKETS // 2
    nf = jnp.maximum(n, 1).astype(f32)
    large = max_exact + (jnp.log(nf / max_exact) / math.log(MAX_DIST / max_exact)
                         * (N_BUCKETS - max_exact)).astype(i32)
    large = jnp.minimum(large, N_BUCKETS - 1)
    return jnp.where(n < max_exact, n, large)


def _bias_of_dist(bias_tab, dist):
    return jnp.moveaxis(bias_tab[jnp.clip(dist, 0, MAX_DIST)], -1, 0)


def _sort_slots(eid, tm):
    n = eid.shape[0]
    a = n * TOP_E
    e_flat = eid.reshape(a)
    onehot = (e_flat[:, None] == jnp.arange(N_EXPERTS, dtype=i32)[None, :]).astype(i32)
    rank = jnp.sum((jnp.cumsum(onehot, axis=0) - onehot) * onehot, axis=1)
    counts = jnp.sum(onehot, axis=0)
    padded = (counts + tm - 1) // tm * tm
    ends = jnp.cumsum(padded)
    starts = ends - padded
    pos = starts[e_flat] + rank
    p_total = (a + tm - 1) // tm * tm + N_EXPERTS * tm
    src_token = jnp.zeros((p_total,), i32).at[pos].set(jnp.arange(a, dtype=i32) // TOP_E)
    tile_start = jnp.arange(p_total // tm, dtype=i32) * tm
    tile_expert = jnp.minimum(jnp.searchsorted(ends, tile_start, side="right"), N_EXPERTS - 1).astype(i32)
    n_used = (ends[-1] // tm).astype(i32).reshape(1)
    pos2 = pos.reshape(n, TOP_E)
    return tile_expert, src_token, n_used, pos2[:, 0], pos2[:, 1]


def _pages_per_chunk(n_pages):
    pp = 16
    while n_pages % pp:
        pp //= 2
    return pp


def kernel(x_prompt, x_sample, cache_k, cache_v, cache_idx_k, cache_ckv, cache_kr, page_table,
           meta_tokens, rel_bias, g_attn, w_in, g_q, w_uq, g_kv, w_uk, w_uv, w_oa, w_ob, w_o,
           g_ffn, w_rg, w_re, w_gate, w_up, w_down, g_final):
    nb, s_len, d = x_prompt.shape
    bs, ts, _ = x_sample.shape
    depth = w_in.shape[0]
    assert depth == 1 and ts <= SUBLANE
    rs = 4 if ts <= 4 else SUBLANE
    pack = SUBLANE // rs
    assert bs % pack == 0
    t_len = s_len + N_META
    tp = -(-t_len // QB) * QB
    n_pages = page_table.shape[1]
    past = n_pages * PAGE
    topk_p = min(TOPK_MAX, s_len // 4)
    topk_s = min(TOPK_MAX, (past + ts) // 4)
    n_prompt = nb * tp
    n_tok = n_prompt + bs * ts
    n_pad = -(-n_tok // ROW_TILE) * ROW_TILE
    tm_proj = next(t for t in (1024, 512, ROW_TILE) if n_pad % t == 0)

    meta = meta_tokens.astype(x_prompt.dtype)
    seq_pad = jnp.zeros((tp - t_len, d), x_prompt.dtype)
    pieces = []
    for b in range(nb):
        pieces += [meta, x_prompt[b], seq_pad]
    h0 = jnp.concatenate(pieces + [x_sample.reshape(bs * ts, d),
                                   jnp.zeros((n_pad - n_tok, d), x_prompt.dtype)], axis=0)
    pos = jnp.concatenate([jnp.tile(jnp.arange(tp, dtype=i32), nb),
                           jnp.tile(past + jnp.arange(ts, dtype=i32), bs),
                           jnp.zeros((n_pad - n_tok,), i32)])
    cos_t, sin_t = _rope_tables(pos)

    l = 0
    w_pack = _pack_w_in(w_in[l], d)
    wuq_p = _pack_w_uq(w_uq[l])
    wuk_t = jnp.transpose(w_uk[l], (1, 2, 0)).astype(bf16)
    wuv_p = jnp.transpose(w_uv[l], (1, 0, 2)).astype(bf16)
    wr = jnp.pad(jnp.concatenate([w_rg[l], w_re[l]], axis=1),
                 ((0, 0), (0, LANE - N_GROUPS - N_EXPERTS))).astype(bf16)

    proj = _norm_matmul(h0, g_attn[l], w_pack, tm_proj, 1024)
    qcat, ckv, kr, kcat, kab, vab, kib = _post_project(proj, cos_t, sin_t, g_q[l], g_kv[l], wuq_p, wuk_t)

    bias_tab = rel_bias[_rel_bucket(jnp.arange(MAX_DIST + 1, dtype=i32))].astype(f32)
    span = QB + BAND
    wvec = bias_tab[jnp.clip(jnp.arange(span, dtype=i32) - (QB - 1), 0, MAX_DIST)].T
    hank = jnp.tile(wvec, (1, QB + 1))[:, :QB * (span + 1)].reshape(H_A, QB, span + 1)
    band = hank[:, :, :BAND][:, :, ::-1]
    far1 = bias_tab[MAX_DIST]
    far = jnp.broadcast_to(far1[:, None, None], (H_A, 1, LANE))
    oa_p = _dsa_prompt(proj, kib, kab, vab, band, far, nb, tp, topk_p)
    oa_p = oa_p.transpose(0, 2, 1).reshape(n_prompt, H_A * DH_A)
    ob_p = _mla_prompt(qcat, kcat, nb, tp)

    pp = _pages_per_chunk(n_pages)
    ch = pp * PAGE
    nbp = bs // pack
    srow = lambda a: a[n_prompt:n_prompt + bs * ts]
    seg = lambda name: srow(proj)[:, _COL[name][0] * _COL[name][1]:(_COL[name][0] + 1) * _COL[name][1]]

    def pack_rows(a, lead):
        feat = a.shape[1:]
        a = a.reshape((nbp, pack, ts) + feat)
        a = jnp.pad(a, ((0, 0), (0, 0), (0, rs - ts)) + ((0, 0),) * len(feat))
        nl = len(lead)
        a = jnp.transpose(a, (0,) + tuple(range(3, 3 + nl)) + (1, 2, 3 + nl))
        return a.reshape((nbp,) + lead + (SUBLANE, feat[-1]))

    qi_s = pack_rows(seg("qi").reshape(bs * ts, H_IDX, D_IDX), (H_IDX,)).reshape(nbp, H_IDX * SUBLANE, D_IDX)
    wi_s = pack_rows(seg("wi")[:, :H_IDX].reshape(bs * ts, H_IDX, 1), (H_IDX,)).reshape(nbp, H_IDX * SUBLANE, 1)
    qa_s = pack_rows(seg("qa").reshape(bs * ts, KVH_A, REP_A, DH_A), (KVH_A, REP_A))
    qa_s = qa_s.reshape(nbp, KVH_A, REP_A * SUBLANE, DH_A)
    nblk_s = -(-(bs * ts) // QB)
    qc_s = qcat[n_prompt // QB:n_prompt // QB + nblk_s].transpose(0, 2, 1, 3)
    qc_s = qc_s.reshape(nblk_s * QB, H_B, MLA_KCAT)[:bs * ts].reshape(bs, ts, H_B, MLA_KCAT)
    qc_s = jnp.pad(qc_s, ((0, 0), (0, SUBLANE - ts), (0, 0), (0, 0))).transpose(0, 2, 1, 3)
    qc_s = qc_s.reshape(bs, H_B * SUBLANE, MLA_KCAT)
    padp = lambda a: jnp.pad(a.reshape(bs, ts, a.shape[-1]), ((0, 0), (0, PAGE - ts), (0, 0)))
    ki_new = padp(srow(kib)[:, :D_IDX])
    kc_new = padp(srow(kcat))
    ka_new = padp(srow(kab))
    va_new = padp(srow(vab))
    idx_t = jnp.swapaxes(cache_idx_k, 2, 3)
    kr_t = jnp.swapaxes(cache_kr, 2, 3)
    scores, ob_s = _sample_pass_a(page_table, qi_s, wi_s * IDX_W_SCALE, qc_s, ki_new, kc_new,
                                  idx_t, cache_ckv, kr_t, ts, rs, pp)
    thr = _sample_thresholds(scores, topk_s)
    j8 = (jnp.arange(SUBLANE, dtype=i32) % rs)[:, None]
    near = min(ch, BAND)
    bias_last = _bias_of_dist(bias_tab, j8 + near - jnp.arange(near, dtype=i32)[None, :])
    if ch > near:
        bias_last = jnp.concatenate(
            [jnp.broadcast_to(far1[:, None, None], (H_A, SUBLANE, ch - near)), bias_last], axis=2)
    bias_new = _bias_of_dist(bias_tab, j8 - jnp.arange(PAGE, dtype=i32)[None, :])
    shp = lambda a: a.reshape((KVH_A, REP_A) + a.shape[1:])
    oa_s = _sample_pass_b(page_table, scores, thr, qa_s, ka_new, va_new, shp(bias_last), shp(bias_new),
                          shp(far1[:, None, None]), cache_k.reshape(-1, PAGE * KVH_A, DH_A),
                          cache_v.reshape(-1, PAGE * KVH_A, DH_A), rs, pp)
    oa_s = oa_s.reshape(nbp, pack, KVH_A, REP_A, pack, rs, DH_A)
    oa_s = jnp.moveaxis(jnp.diagonal(oa_s, axis1=1, axis2=4), -1, 1)
    oa_s = oa_s[:, :, :, :, :ts].transpose(0, 1, 4, 2, 3, 5).reshape(bs * ts, H_A * DH_A).astype(bf16)
    ob_s = ob_s.reshape(bs, H_B, SUBLANE, KV_LORA)[:, :, :ts].transpose(0, 2, 1, 3)
    ob_s = ob_s.reshape(bs * ts, H_B * KV_LORA).astype(bf16)
    tail = lambda w: jnp.zeros((n_pad - n_tok, w), bf16)
    oa = jnp.concatenate([oa_p, oa_s, tail(H_A * DH_A)], axis=0)
    ob = jnp.concatenate([ob_p, ob_s, tail(H_B * KV_LORA)], axis=0)

    h1, xn3, route = _merge(oa, ob, proj, h0, w_oa[l].astype(bf16), wuv_p, w_ob[l].astype(bf16),
                            w_o[l].astype(bf16), g_ffn[l], wr)
    eid = route[:, 0:TOP_E].astype(i32)
    tile_expert, src_token, n_used, pos0, pos1 = _sort_slots(eid, ROW_TILE)
    y_sorted = _moe_experts(tile_expert, src_token, n_used, xn3, w_gate[l], w_up[l], w_down[l], ROW_TILE)
    y = _combine(pos0, pos1, h1, route, g_final, y_sorted)

    y_prompt = jnp.stack([y[b * tp + N_META:b * tp + t_len] for b in range(nb)])
    y_sample = y[n_prompt:n_tok].reshape(bs, ts, d)
    seg_all = lambda name: proj[:, _COL[name][0] * _COL[name][1]:(_COL[name][0] + 1) * _COL[name][1]]

    def states(rows, lead):
        ka = rows(seg_all("ka")).reshape(lead + (KVH_A, DH_A))
        va = rows(seg_all("va")).reshape(lead + (KVH_A, DH_A))
        ki = rows(seg_all("ki"))[..., :D_IDX].reshape(lead + (D_IDX,))
        cc = rows(ckv).reshape(lead + (KV_LORA,))
        rr = rows(kr)[..., :D_ROPE].reshape(lead + (D_ROPE,))
        return [a[None] for a in (ka, va, ki, cc, rr)]

    prow = lambda a: a[:n_prompt].reshape(nb, tp, a.shape[-1])[:, :t_len]
    st_p = states(prow, (nb, t_len))
    st_s = states(lambda a: a[n_prompt:n_tok], (bs, ts))
    return (y_prompt, y_sample, *st_p, *st_s)
```
